```python
import math
import jax, jax.numpy as jnp
from jax import lax
import numpy as np

D_MODEL = 2048
BATCH = 2
SEQ = 8192
DEPTH = 1
DEC_BATCH = 2
DEC_SEQ = 16384
PAST_LEN = 128

N_HEADS_ATTN = 16
Q_LORA = 512
KV_LORA = 512
QK_NOPE = 128
QK_ROPE = 64
V_HEAD = 128
QK_HEAD = QK_NOPE + QK_ROPE
ROPE_THETA = 10000.0
Q_BLOCK = 128
SSM_EXPAND = 2
D_INNER = SSM_EXPAND * D_MODEL
SSM_HEAD_DIM = 64
N_HEADS_SSM = D_INNER // SSM_HEAD_DIM
N_GROUPS = 8
HEADS_PER_GROUP = N_HEADS_SSM // N_GROUPS
D_STATE = 128
CONV_K = 5
CONV_DIM = D_INNER + 2 * N_GROUPS * D_STATE
SSD_CHUNK = 128
N_EXPERTS = 32
TOP_K = 4
D_FF = D_MODEL
SWIGLU_LIMIT = 7.0
SWIGLU_ALPHA = 1.702
MOE_BLOCK = 128
EPS = 1e-6
OFF_KV = Q_LORA
OFF_KR = OFF_KV + KV_LORA
OFF_Z = OFF_KR + QK_ROPE
OFF_XBC = OFF_Z + D_INNER
OFF_DTF = OFF_XBC + CONV_DIM
OFF_DTB = OFF_DTF + N_HEADS_SSM
OFF_GA = OFF_DTB + N_HEADS_SSM
OFF_GS = OFF_GA + D_MODEL
D_IN_PROJ = OFF_GS + D_MODEL

kernel_name = "hybrid_mla_ssd_moe_encoder"


def _rms(x, w):
    xf = x.astype(jnp.float32)
    y = xf * lax.rsqrt(jnp.mean(xf * xf, axis=-1, keepdims=True) + EPS)
    return y.astype(x.dtype) * w


def _rope_tables(seq):
    inv_freq = 1.0 / (ROPE_THETA ** (jnp.arange(0, QK_ROPE, 2, dtype=jnp.float32) / QK_ROPE))
    ang = jnp.arange(seq, dtype=jnp.float32)[:, None] * inv_freq[None, :]
    ang = jnp.concatenate([ang, ang], axis=-1)
    return jnp.cos(ang), jnp.sin(ang)


def _apply_rope(x, cos, sin):
    xf = x.astype(jnp.float32)
    half = QK_ROPE // 2
    rot = jnp.concatenate([-xf[..., half:], xf[..., :half]], axis=-1)
    return (xf * cos + rot * sin).astype(x.dtype)


def _mla(c_q, c_kv, k_rope_raw, q_norm_w, kv_norm_w, w_uq, w_ukv):
    b, s, _ = c_q.shape
    q = (_rms(c_q, q_norm_w) @ w_uq).reshape(b, s, N_HEADS_ATTN, QK_HEAD)
    kv = (_rms(c_kv, kv_norm_w) @ w_ukv).reshape(b, s, N_HEADS_ATTN, QK_NOPE + V_HEAD)
    q_nope, q_rope = q[..., :QK_NOPE], q[..., QK_NOPE:]
    k_nope, v = kv[..., :QK_NOPE], kv[..., QK_NOPE:]
    cos, sin = _rope_tables(s)
    q_rope = _apply_rope(q_rope, cos[:, None, :], sin[:, None, :])
    k_rope = _apply_rope(k_rope_raw, cos, sin)
    scale = QK_HEAD ** -0.5
    nb = s // Q_BLOCK
    qn_b = q_nope.reshape(b, nb, Q_BLOCK, N_HEADS_ATTN, QK_NOPE).transpose(1, 0, 2, 3, 4)
    qr_b = q_rope.reshape(b, nb, Q_BLOCK, N_HEADS_ATTN, QK_ROPE).transpose(1, 0, 2, 3, 4)

    def q_block(args):
        qn, qr = args
        sc = (jnp.einsum('bqhd,bkhd->bhqk', qn, k_nope)
              + jnp.einsum('bqhd,bkd->bhqk', qr, k_rope))
        p = jax.nn.softmax(sc.astype(jnp.float32) * scale, axis=-1).astype(v.dtype)
        return jnp.einsum('bhqk,bkhd->bqhd', p, v)

    o = lax.map(q_block, (qn_b, qr_b))
    return o.transpose(1, 0, 2, 3, 4).reshape(b, s, N_HEADS_ATTN * V_HEAD)


def _ssd_scan(x, dt, a_coef, bm, cm):
    bsz, s = x.shape[0], x.shape[1]
    nc, L = s // SSD_CHUNK, SSD_CHUNK
    G, R, P, N = N_GROUPS, HEADS_PER_GROUP, SSM_HEAD_DIM, D_STATE
    xc = x.reshape(bsz, nc, L, G, R, P).transpose(1, 0, 2, 3, 4, 5)
    dtc = dt.reshape(bsz, nc, L, G, R).transpose(1, 0, 2, 3, 4)
    bc = bm.reshape(bsz, nc, L, G, N).transpose(1, 0, 2, 3, 4)
    cc = cm.reshape(bsz, nc, L, G, N).transpose(1, 0, 2, 3, 4)
    a_gr = a_coef.reshape(G, R)
    tril = jnp.tril(jnp.ones((L, L), dtype=bool))[None, :, :, None, None]

    def step(state, inp):
        xk, dtk, bk, ck = inp
        acum = jnp.cumsum(dtk * a_gr, axis=1)
        seg = acum[:, :, None] - acum[:, None, :]
        decay = jnp.exp(jnp.where(tril, seg, -jnp.inf))
        cb = jnp.einsum('blgn,bsgn->blsg', ck, bk)
        y_intra = jnp.einsum('blsg,blsgr,bsgrp->blgrp', cb, decay, xk * dtk[..., None])
        y_inter = jnp.einsum('blgn,bgrpn->blgrp', ck, state) * jnp.exp(acum)[..., None]
        a_last = acum[:, -1]
        w_end = jnp.exp(a_last[:, None] - acum) * dtk
        new_state = (state * jnp.exp(a_last)[..., None, None]
                     + jnp.einsum('bsgn,bsgr,bsgrp->bgrpn', bk, w_end, xk))
        return new_state, y_intra + y_inter

    state0 = jnp.zeros((bsz, G, R, P, N), jnp.float32)
    _, y = lax.scan(step, state0, (xc, dtc, bc, cc))
    return y.transpose(1, 0, 2, 3, 4, 5).reshape(bsz, s, N_HEADS_SSM, P)


def _ssm(z, xbc_raw, dt_f_raw, dt_b_raw, conv_w, conv_b, dt_bias_f, dt_bias_b,
         a_log_f, a_log_b, d_skip, ssm_norm_w):
    b, s, _ = z.shape
    xbc = lax.conv_general_dilated(
        xbc_raw, conv_w.reshape(CONV_K, 1, CONV_DIM), window_strides=(1,),
        padding=[(CONV_K // 2, CONV_K // 2)], dimension_numbers=('NWC', 'WIO', 'NWC'),
        feature_group_count=CONV_DIM) + conv_b
    xbc = jax.nn.silu(xbc).astype(jnp.float32)
    xs = xbc[..., :D_INNER].reshape(b, s, N_HEADS_SSM, SSM_HEAD_DIM)
    bm = xbc[..., D_INNER:D_INNER + N_GROUPS * D_STATE].reshape(b, s, N_GROUPS, D_STATE)
    cm = xbc[..., D_INNER + N_GROUPS * D_STATE:].reshape(b, s, N_GROUPS, D_STATE)
    dt_f = jax.nn.softplus(dt_f_raw.astype(jnp.float32) + dt_bias_f.astype(jnp.float32))
    dt_b = jax.nn.softplus(dt_b_raw.astype(jnp.float32) + dt_bias_b.astype(jnp.float32))
    a_f = -jnp.exp(a_log_f.astype(jnp.float32))
    a_b = -jnp.exp(a_log_b.astype(jnp.float32))
    y_f = _ssd_scan(xs, dt_f, a_f, bm, cm)
    y_b = jnp.flip(_ssd_scan(jnp.flip(xs, 1), jnp.flip(dt_b, 1), a_b,
                             jnp.flip(bm, 1), jnp.flip(cm, 1)), 1)
    y = y_f + y_b + d_skip.astype(jnp.float32)[:, None] * xs
    y = y.reshape(b, s, D_INNER) * jax.nn.silu(z.astype(jnp.float32))
    yg = y.reshape(b, s, N_GROUPS, D_INNER // N_GROUPS)
    yg = yg * lax.rsqrt(jnp.mean(yg * yg, axis=-1, keepdims=True) + EPS)
    return yg.reshape(b, s, D_INNER).astype(z.dtype) * ssm_norm_w


def _moe(xf, w_router, b_router, w_gu, b_gu, w_down, b_down):
    n = xf.shape[0]
    nk = n * TOP_K
    nblk = -(-(nk + N_EXPERTS * (MOE_BLOCK - 1)) // MOE_BLOCK)
    rows = nblk * MOE_BLOCK
    logits = (xf @ w_router + b_router).astype(jnp.float32)
    top_v, top_i = lax.top_k(logits, TOP_K)
    gates = jax.nn.softmax(top_v, axis=-1).astype(xf.dtype)
    flat_e = top_i.reshape(-1)
    flat_tok = jnp.arange(nk, dtype=jnp.int32) // TOP_K
    flat_g = gates.reshape(-1)
    order = jnp.argsort(flat_e)
    e_sorted = flat_e[order]
    counts = jnp.zeros((N_EXPERTS,), jnp.int32).at[flat_e].add(1)
    starts = jnp.cumsum(counts) - counts
    padded = (counts + MOE_BLOCK - 1) // MOE_BLOCK * MOE_BLOCK
    pad_ends = jnp.cumsum(padded)
    pad_starts = pad_ends - padded
    dest = pad_starts[e_sorted] + (jnp.arange(nk, dtype=jnp.int32) - starts[e_sorted])
    buf_tok = jnp.full((rows,), n, jnp.int32).at[dest].set(flat_tok[order])
    buf_gate = jnp.zeros((rows,), xf.dtype).at[dest].set(flat_g[order])
    blk_e = jnp.minimum(jnp.searchsorted(pad_ends, jnp.arange(nblk, dtype=jnp.int32) * MOE_BLOCK,
                                         side='right'), N_EXPERTS - 1)
    x_pad = jnp.concatenate([xf, jnp.zeros((1, xf.shape[1]), xf.dtype)], axis=0)
    xb = x_pad[buf_tok].reshape(nblk, MOE_BLOCK, xf.shape[1])

    def expert_block(args):
        xk, e = args
        gu = xk @ w_gu[e] + b_gu[e]
        g = jnp.minimum(gu[:, :D_FF], SWIGLU_LIMIT)
        u = jnp.clip(gu[:, D_FF:], -SWIGLU_LIMIT, SWIGLU_LIMIT)
        h = g * jax.nn.sigmoid(SWIGLU_ALPHA * g) * (u + 1.0)
        return h @ w_down[e] + b_down[e]

    yb = lax.map(expert_block, (xb, blk_e)).reshape(rows, xf.shape[1])
    y = jax.ops.segment_sum(yb * buf_gate[:, None], buf_tok, num_segments=n + 1)
    return y[:n]


def _trunk(x, norm_mix_w, w_in, q_norm_w, kv_norm_w, w_uq, w_ukv, conv_w, conv_b,
           dt_bias_f, dt_bias_b, a_log_f, a_log_b, d_skip, ssm_norm_w, w_br_attn, w_br_ssm,
           w_out, norm_ffn_w, w_router, b_router, w_gu, b_gu, w_down, b_down, norm_final_w):
    b, s, d = x.shape
    h = x
    for l in range(DEPTH):
        xn = _rms(h, norm_mix_w[l])
        proj = xn @ w_in[l]
        c_q, c_kv, k_rope, z, xbc, dt_f, dt_b, g_a, g_s = jnp.split(
            proj, [OFF_KV, OFF_KR, OFF_Z, OFF_XBC, OFF_DTF, OFF_DTB, OFF_GA, OFF_GS], axis=-1)
        o_attn = _mla(c_q, c_kv, k_rope, q_norm_w[l], kv_norm_w[l], w_uq[l], w_ukv[l]) @ w_br_attn[l]
        o_ssm = _ssm(z, xbc, dt_f, dt_b, conv_w[l], conv_b[l], dt_bias_f[l], dt_bias_b[l],
                     a_log_f[l], a_log_b[l], d_skip[l], ssm_norm_w[l]) @ w_br_ssm[l]
        merged = jax.nn.sigmoid(g_a) * o_attn + jax.nn.sigmoid(g_s) * o_ssm
        h = h + merged @ w_out[l]
        hn = _rms(h, norm_ffn_w[l]).reshape(b * s, d)
        h = h + _moe(hn, w_router[l], b_router[l], w_gu[l], b_gu[l], w_down[l],
                     b_down[l]).reshape(b, s, d)
    return _rms(h, norm_final_w)


def setup_inputs(seed: int = 0) -> dict:
    key = jax.random.key(seed)
    ks = jax.random.split(key, 32)
    f32 = jnp.float32

    def nrm(k, shape, fan_in):
        return jax.random.normal(k, shape, f32) * (fan_in ** -0.5)

    def gain(k, shape):
        return 1.0 + 0.01 * jax.random.normal(k, shape, f32)

    def small(k, shape):
        return 0.01 * jax.random.normal(k, shape, f32)

    def dt_bias(k):
        dt = jnp.exp(jax.random.uniform(k, (DEPTH, N_HEADS_SSM), f32, math.log(0.001), math.log(0.1)))
        return dt + jnp.log(-jnp.expm1(-dt))

    return {
        "x_prompt": jax.random.normal(ks[0], (BATCH, SEQ, D_MODEL), f32),
        "x_sample": jax.random.normal(ks[1], (DEC_BATCH, DEC_SEQ, D_MODEL), f32),
        "norm_mix_w": gain(ks[2], (DEPTH, D_MODEL)),
        "w_in": nrm(ks[3], (DEPTH, D_MODEL, D_IN_PROJ), D_MODEL),
        "q_norm_w": gain(ks[4], (DEPTH, Q_LORA)),
        "kv_norm_w": gain(ks[5], (DEPTH, KV_LORA)),
        "w_uq": nrm(ks[6], (DEPTH, Q_LORA, N_HEADS_ATTN * QK_HEAD), Q_LORA),
        "w_ukv": nrm(ks[7], (DEPTH, KV_LORA, N_HEADS_ATTN * (QK_NOPE + V_HEAD)), KV_LORA),
        "conv_w": nrm(ks[8], (DEPTH, CONV_K, CONV_DIM), CONV_K),
        "conv_b": small(ks[9], (DEPTH, CONV_DIM)),
        "dt_bias_f": dt_bias(ks[10]),
        "dt_bias_b": dt_bias(ks[11]),
        "a_log_f": jnp.log(jax.random.uniform(ks[12], (DEPTH, N_HEADS_SSM), f32, 1.0, 16.0)),
        "a_log_b": jnp.log(jax.random.uniform(ks[13], (DEPTH, N_HEADS_SSM), f32, 1.0, 16.0)),
        "d_skip": 1.0 + 0.1 * jax.random.normal(ks[14], (DEPTH, N_HEADS_SSM), f32),
        "ssm_norm_w": gain(ks[15], (DEPTH, D_INNER)),
        "w_br_attn": nrm(ks[16], (DEPTH, N_HEADS_ATTN * V_HEAD, D_MODEL), N_HEADS_ATTN * V_HEAD),
        "w_br_ssm": nrm(ks[17], (DEPTH, D_INNER, D_MODEL), D_INNER),
        "w_out": nrm(ks[18], (DEPTH, D_MODEL, D_MODEL), D_MODEL),
        "norm_ffn_w": gain(ks[19], (DEPTH, D_MODEL)),
        "w_router": nrm(ks[20], (DEPTH, D_MODEL, N_EXPERTS), D_MODEL),
        "b_router": small(ks[21], (DEPTH, N_EXPERTS)),
        "w_gu": nrm(ks[22], (DEPTH, N_EXPERTS, D_MODEL, 2 * D_FF), D_MODEL),
        "b_gu": small(ks[23], (DEPTH, N_EXPERTS, 2 * D_FF)),
        "w_down": nrm(ks[24], (DEPTH, N_EXPERTS, D_FF, D_MODEL), D_FF),
        "b_down": small(ks[25], (DEPTH, N_EXPERTS, D_MODEL)),
        "norm_final_w": gain(ks[26], (D_MODEL,)),
    }


def reference(x_prompt, x_sample, norm_mix_w, w_in, q_norm_w, kv_norm_w, w_uq, w_ukv, conv_w,
              conv_b, dt_bias_f, dt_bias_b, a_log_f, a_log_b, d_skip, ssm_norm_w, w_br_attn,
              w_br_ssm, w_out, norm_ffn_w, w_router, b_router, w_gu, b_gu, w_down, b_down,
              norm_final_w):
    y_prompt = _trunk(x_prompt, norm_mix_w, w_in, q_norm_w, kv_norm_w, w_uq, w_ukv, conv_w, conv_b,
                      dt_bias_f, dt_bias_b, a_log_f, a_log_b, d_skip, ssm_norm_w, w_br_attn,
                      w_br_ssm, w_out, norm_ffn_w, w_router, b_router, w_gu, b_gu, w_down,
                      b_down, norm_final_w)
    y_sample = _trunk(x_sample, norm_mix_w, w_in, q_norm_w, kv_norm_w, w_uq, w_ukv, conv_w, conv_b,
                      dt_bias_f, dt_bias_b, a_log_f, a_log_b, d_skip, ssm_norm_w, w_br_attn,
                      w_br_ssm, w_out, norm_ffn_w, w_router, b_router, w_gu, b_gu, w_down,
                      b_down, norm_final_w)
    return (y_prompt, y_sample)
```

```python
import functools
import math
from typing import NamedTuple

import jax
import jax.numpy as jnp
from jax import lax
from jax.experimental import pallas as pl
from jax.experimental.pallas import tpu as pltpu

F32 = jnp.float32
BF16 = jnp.bfloat16
LANES = 128
HALF = 64


class Cfg(NamedTuple):
    d_model: int = 2048
    n_heads: int = 16
    q_lora: int = 512
    kv_lora: int = 512
    qk_nope: int = 128
    qk_rope: int = 64
    v_head: int = 128
    rope_theta: float = 10000.0
    d_inner: int = 4096
    ssm_head_dim: int = 64
    n_groups: int = 8
    d_state: int = 128
    conv_k: int = 5
    chunk: int = 128
    n_experts: int = 32
    top_k: int = 4
    d_ff: int = 2048
    limit: float = 7.0
    alpha: float = 1.702
    eps: float = 1e-6
    tm: int = 512
    tq: int = 512
    tk: int = 512
    t_conv: int = 512
    tc_conv: int = 512
    tm_gate: int = 256
    tm_out: int = 256
    tm_route: int = 512
    tm_moe: int = 512
    tf_moe: int = 512
    tm_comb: int = 256


def _cp(sem, vmem_mb=None):
    kw = dict(dimension_semantics=sem)
    if vmem_mb is not None:
        kw["vmem_limit_bytes"] = vmem_mb * 1024 * 1024
    return pltpu.CompilerParams(**kw)


def _rms_mm_kernel(x_ref, nw_ref, w_ref, *rest, eps, tab_reps):
    if tab_reps:
        tab_ref, o_ref, xn_ref = rest
    else:
        o_ref, xn_ref = rest

    @pl.when(pl.program_id(1) == 0)
    def _():
        x = x_ref[...].astype(F32)
        ms = jnp.mean(x * x, axis=-1, keepdims=True)
        xn_ref[...] = (x * lax.rsqrt(ms + eps) * nw_ref[...]).astype(BF16)

    acc = jnp.dot(xn_ref[...], w_ref[...], preferred_element_type=F32)
    if tab_reps:
        t = tab_ref[...]
        if tab_reps > 1:
            t = jnp.concatenate([t] * tab_reps, axis=1)
        acc = acc * t
    o_ref[...] = acc.astype(o_ref.dtype)


def _rms_matmul(x, x_colblk, k, norm_w, w, out_dtype, *, tm, tn, eps, tab=None, seq=None):
    n = x.shape[0]
    nout = w.shape[1]
    assert n % tm == 0 and nout % tn == 0 and w.shape[0] == k
    in_specs = [
        pl.BlockSpec((tm, k), lambda i, j: (i, x_colblk)),
        pl.BlockSpec((1, k), lambda i, j: (0, 0)),
        pl.BlockSpec((k, tn), lambda i, j: (0, j)),
    ]
    args = [x, norm_w.reshape(1, k).astype(F32), w]
    tab_reps = 0
    if tab is not None:
        tw = tab.shape[1]
        assert tn % tw == 0 and seq % tm == 0
        tab_reps = tn // tw
        nst = seq // tm
        in_specs.append(pl.BlockSpec((tm, tw), lambda i, j: (i % nst, 0)))
        args.append(tab)
    return pl.pallas_call(
        functools.partial(_rms_mm_kernel, eps=eps, tab_reps=tab_reps),
        grid=(n // tm, nout // tn),
        in_specs=in_specs,
        out_specs=pl.BlockSpec((tm, tn), lambda i, j: (i, j)),
        out_shape=jax.ShapeDtypeStruct((n, nout), out_dtype),
        scratch_shapes=[pltpu.VMEM((tm, k), BF16)],
        compiler_params=_cp(("parallel", "arbitrary"), 48),
    )(*args)


HALO = 16


def _conv_kernel(prev_ref, x_ref, next_ref, w_ref, b_ref, o_ref, buf_ref, *, t, nt, kk):
    ti = pl.program_id(1)
    pv = prev_ref[...].astype(F32)[HALO - 8:, :]
    nx = next_ref[...].astype(F32)[:8, :]
    pv = jnp.where(ti == 0, 0.0, pv)
    nx = jnp.where(ti == nt - 1, 0.0, nx)
    buf_ref[0:8, :] = pv
    buf_ref[8:t + 8, :] = x_ref[...].astype(F32)
    buf_ref[t + 8:t + 16, :] = nx
    acc = b_ref[...]
    half = kk // 2
    for k in range(kk):
        acc = acc + w_ref[k:k + 1, :] * buf_ref[pl.ds(8 - half + k, t), :]
    o_ref[...] = (acc * jax.nn.sigmoid(acc)).astype(o_ref.dtype)


def _conv_silu(proj_b, col0, conv_w, conv_b, *, bsz, seq, cfg):
    n = proj_b.shape[0]
    cdim = conv_w.shape[1]
    t, tc = min(cfg.t_conv, seq), cfg.tc_conv
    assert seq % t == 0 and cdim % tc == 0 and col0 % tc == 0 and t % HALO == 0
    nt = seq // t
    cb0 = col0 // tc
    nrb = n // HALO
    rb = t // HALO

    def prev_map(b, ti, j):
        return (jnp.maximum((b * nt + ti) * rb - 1, 0), cb0 + j)

    def next_map(b, ti, j):
        return (jnp.minimum((b * nt + ti + 1) * rb, nrb - 1), cb0 + j)

    return pl.pallas_call(
        functools.partial(_conv_kernel, t=t, nt=nt, kk=cfg.conv_k),
        grid=(bsz, nt, cdim // tc),
        in_specs=[
            pl.BlockSpec((HALO, tc), prev_map),
            pl.BlockSpec((t, tc), lambda b, ti, j: (b * nt + ti, cb0 + j)),
            pl.BlockSpec((HALO, tc), next_map),
            pl.BlockSpec((cfg.conv_k, tc), lambda b, ti, j: (0, j)),
            pl.BlockSpec((1, tc), lambda b, ti, j: (0, j)),
        ],
        out_specs=pl.BlockSpec((t, tc), lambda b, ti, j: (b * nt + ti, j)),
        out_shape=jax.ShapeDtypeStruct((n, cdim), BF16),
        scratch_shapes=[pltpu.VMEM((t + 16, tc), F32)],
        compiler_params=_cp(("parallel", "parallel", "parallel")),
    )(proj_b, proj_b, proj_b, conv_w.astype(F32), conv_b.reshape(1, cdim).astype(F32))


def _ssd_kernel(xs_ref, b_ref, c_ref, dt_ref, bias_ref, a_ref, y_ref, st_ref, *,
                reverse, ll, ng, rr, pp, nn):
    assert pp == HALF and rr % 2 == 0 and 2 * ng * rr == LANES
    ci = pl.program_id(1)

    @pl.when(ci == 0)
    def _():
        st_ref[...] = jnp.zeros_like(st_ref)

    dcol = ng * rr if reverse else 0
    z = dt_ref[...] + bias_ref[...]
    dt_all = jnp.maximum(z, 0.0) + jnp.log(1.0 + jnp.exp(-jnp.abs(z)))
    dta = dt_all * a_ref[...]
    row = lax.broadcasted_iota(jnp.int32, (ll, ll), 0)
    col = lax.broadcasted_iota(jnp.int32, (ll, ll), 1)
    mask = (row <= col) if reverse else (row >= col)
    tri = mask.astype(F32)
    acum = jnp.dot(tri, dta, preferred_element_type=F32, precision=lax.Precision.HIGHEST)
    total = acum[0:1, :] if reverse else acum[ll - 1:ll, :]
    eac = jnp.exp(acum)
    wend = jnp.exp(total - acum) * dt_all
    etot = jnp.exp(total)
    acum_t = acum.T
    dt_t = dt_all.T
    lane = lax.broadcasted_iota(jnp.int32, (ll, LANES), 1)
    lo = lane < HALF
    lane1 = lax.broadcasted_iota(jnp.int32, (1, LANES), 1)
    lo1 = lane1 < HALF

    for g in range(ng):
        bg = b_ref[:, g * nn:(g + 1) * nn]
        cg = c_ref[:, g * nn:(g + 1) * nn]
        cb = lax.dot_general(cg, bg, (((1,), (1,)), ((), ())), preferred_element_type=F32)
        st = st_ref[g]
        gw = rr * pp
        yint = jnp.dot(cg, st.astype(BF16), preferred_element_type=F32)
        xw_parts = []
        et_parts = []
        for pr in range(rr // 2):
            c0 = dcol + g * rr + 2 * pr
            lanes0 = g * gw + pr * LANES
            xp = xs_ref[:, lanes0:lanes0 + LANES]
            ys = []
            for c in (c0, c0 + 1):
                seg = acum[:, c:c + 1] - acum_t[c:c + 1, :]
                dec = jnp.exp(jnp.where(mask, seg, -jnp.inf))
                m = (cb * dec * dt_t[c:c + 1, :]).astype(BF16)
                ys.append(jnp.dot(m, xp, preferred_element_type=F32))
            e_sel = jnp.where(lo, eac[:, c0:c0 + 1], eac[:, c0 + 1:c0 + 2])
            y = jnp.where(lo, ys[0], ys[1]) + yint[:, pr * LANES:(pr + 1) * LANES] * e_sel
            y_ref[:, lanes0:lanes0 + LANES] = y.astype(y_ref.dtype)
            w_sel = jnp.where(lo, wend[:, c0:c0 + 1], wend[:, c0 + 1:c0 + 2])
            xw_parts.append((xp.astype(F32) * w_sel).astype(BF16))
            et_parts.append(jnp.where(lo1, etot[:, c0:c0 + 1], etot[:, c0 + 1:c0 + 2]))
        xw = jnp.concatenate(xw_parts, axis=1)
        et = jnp.concatenate(et_parts, axis=1)
        upd = lax.dot_general(bg, xw, (((0,), (0,)), ((), ())), preferred_element_type=F32)
        st_ref[g] = st * et + upd


def _ssd_scan(xbc, dt_src, dt_colblk, bias2, a2, *, reverse, bsz, seq, cfg):
    n = xbc.shape[0]
    ll, ng, pp, nn = cfg.chunk, cfg.n_groups, cfg.ssm_head_dim, cfg.d_state
    di = cfg.d_inner
    rr = di // pp // ng
    nc = seq // ll
    gn = ng * nn
    assert di % gn == 0
    boff = di // gn

    def rblk(b, c):
        return b * nc + ((nc - 1 - c) if reverse else c)

    return pl.pallas_call(
        functools.partial(_ssd_kernel, reverse=reverse, ll=ll, ng=ng, rr=rr, pp=pp, nn=nn),
        grid=(bsz, nc),
        in_specs=[
            pl.BlockSpec((ll, di), lambda b, c: (rblk(b, c), 0)),
            pl.BlockSpec((ll, gn), lambda b, c: (rblk(b, c), boff)),
            pl.BlockSpec((ll, gn), lambda b, c: (rblk(b, c), boff + 1)),
            pl.BlockSpec((ll, LANES), lambda b, c: (rblk(b, c), dt_colblk)),
            pl.BlockSpec((1, LANES), lambda b, c: (0, 0)),
            pl.BlockSpec((1, LANES), lambda b, c: (0, 0)),
        ],
        out_specs=pl.BlockSpec((ll, di), lambda b, c: (rblk(b, c), 0)),
        out_shape=jax.ShapeDtypeStruct((n, di), BF16),
        scratch_shapes=[pltpu.VMEM((ng, nn, rr * pp), F32)],
        compiler_params=_cp(("parallel", "arbitrary")),
    )(xbc, xbc, xbc, dt_src, bias2, a2)


def _ssm_out_kernel(yf_ref, yb_ref, xs_ref, z_ref, dsk_ref, nw_ref, w_ref, gs_ref, oa_ref,
                    o_ref, yn_ref, *, eps, ng):
    @pl.when(pl.program_id(1) == 0)
    def _():
        di = yn_ref.shape[1]
        gw = di // ng
        for g in range(ng):
            sl = slice(g * gw, (g + 1) * gw)
            y = (yf_ref[:, sl].astype(F32) + yb_ref[:, sl].astype(F32)
                 + dsk_ref[:, sl] * xs_ref[:, sl].astype(F32))
            zz = z_ref[:, sl].astype(F32)
            y = y * (zz * jax.nn.sigmoid(zz))
            ms = jnp.mean(y * y, axis=-1, keepdims=True)
            yn_ref[:, sl] = (y * lax.rsqrt(ms + eps) * nw_ref[:, sl]).astype(BF16)

    acc = jnp.dot(yn_ref[...], w_ref[...], preferred_element_type=F32)
    gate = jax.nn.sigmoid(gs_ref[...].astype(F32))
    o_ref[...] = (oa_ref[...].astype(F32) + gate * acc).astype(o_ref.dtype)


def _ssm_out(yf, yb, xbc, proj_b, gs_col0, dskip_x, norm_w, w_br, oa, *, cfg):
    n = yf.shape[0]
    di, dm = cfg.d_inner, cfg.d_model
    tm, tn = cfg.tm_gate, min(512, dm)
    assert n % tm == 0 and dm % tn == 0 and gs_col0 % tn == 0
    gsb = gs_col0 // tn
    return pl.pallas_call(
        functools.partial(_ssm_out_kernel, eps=cfg.eps, ng=cfg.n_groups),
        grid=(n // tm, dm // tn),
        in_specs=[
            pl.BlockSpec((tm, di), lambda i, j: (i, 0)),
            pl.BlockSpec((tm, di), lambda i, j: (i, 0)),
            pl.BlockSpec((tm, di), lambda i, j: (i, 0)),
            pl.BlockSpec((tm, di), lambda i, j: (i, 0)),
            pl.BlockSpec((1, di), lambda i, j: (0, 0)),
            pl.BlockSpec((1, di), lambda i, j: (0, 0)),
            pl.BlockSpec((di, tn), lambda i, j: (0, j)),
            pl.BlockSpec((tm, tn), lambda i, j: (i, gsb + j)),
            pl.BlockSpec((tm, tn), lambda i, j: (i, j)),
        ],
        out_specs=pl.BlockSpec((tm, tn), lambda i, j: (i, j)),
        out_shape=jax.ShapeDtypeStruct((n, dm), BF16),
        scratch_shapes=[pltpu.VMEM((tm, di), BF16)],
        compiler_params=_cp(("parallel", "arbitrary"), 48),
    )(yf, yb, xbc, proj_b, dskip_x, norm_w, w_br, proj_b, oa)


def _gated_mm_kernel(x_ref, w_ref, g_ref, o_ref):
    acc = jnp.dot(x_ref[...], w_ref[...], preferred_element_type=F32)
    o_ref[...] = (jax.nn.sigmoid(g_ref[...].astype(F32)) * acc).astype(o_ref.dtype)


def _gated_matmul(x, w, gate_src, gate_col0, *, tm, tn):
    n, k = x.shape
    nout = w.shape[1]
    assert n % tm == 0 and nout % tn == 0 and gate_col0 % tn == 0
    gb = gate_col0 // tn
    return pl.pallas_call(
        _gated_mm_kernel,
        grid=(n // tm, nout // tn),
        in_specs=[
            pl.BlockSpec((tm, k), lambda i, j: (i, 0)),
            pl.BlockSpec((k, tn), lambda i, j: (0, j)),
            pl.BlockSpec((tm, tn), lambda i, j: (i, gb + j)),
        ],
        out_specs=pl.BlockSpec((tm, tn), lambda i, j: (i, j)),
        out_shape=jax.ShapeDtypeStruct((n, nout), BF16),
        compiler_params=_cp(("parallel", "arbitrary"), 48),
    )(x, w, gate_src)


def _kdup_kernel(kr_ref, cs_ref, o_ref):
    k2 = kr_ref[...].astype(F32) * cs_ref[...]
    o_ref[...] = (k2 + pltpu.roll(k2, HALF, 1)).astype(o_ref.dtype)


def _kdup(proj_a, colblk, cs, *, seq, tm):
    n = proj_a.shape[0]
    tm = min(tm, seq)
    nst = seq // tm
    return pl.pallas_call(
        _kdup_kernel,
        grid=(n // tm,),
        in_specs=[
            pl.BlockSpec((tm, LANES), lambda i: (i, colblk)),
            pl.BlockSpec((tm, LANES), lambda i: (i % nst, 0)),
        ],
        out_specs=pl.BlockSpec((tm, LANES), lambda i: (i, 0)),
        out_shape=jax.ShapeDtypeStruct((n, LANES), BF16),
        compiler_params=_cp(("parallel",)),
    )(proj_a, cs)


def _flash_kernel(q_ref, kn_ref, kd_ref, v_ref, o_ref, *, tk, nk):
    q = q_ref[...]
    tq = q.shape[0]
    dv = v_ref.shape[1]

    def body(c, carry):
        m, l, acc = carry
        off = pl.multiple_of(c * tk, tk)
        kcat = jnp.concatenate([kn_ref[pl.ds(off, tk), :], kd_ref[pl.ds(off, tk), :]], axis=1)
        s = lax.dot_general(q, kcat, (((1,), (1,)), ((), ())), preferred_element_type=F32)
        m_new = jnp.maximum(m, jnp.max(s, axis=1, keepdims=True))
        alpha = jnp.exp2(m - m_new)
        p = jnp.exp2(s - m_new)
        l = alpha * l + jnp.sum(p, axis=1, keepdims=True)
        acc = alpha * acc + jnp.dot(p.astype(BF16), v_ref[pl.ds(off, tk), :],
                                    preferred_element_type=F32)
        return m_new, l, acc

    m0 = jnp.full((tq, 1), -jnp.inf, F32)
    l0 = jnp.zeros((tq, 1), F32)
    a0 = jnp.zeros((tq, dv), F32)
    _, l, acc = lax.fori_loop(0, nk, body, (m0, l0, a0))
    o_ref[...] = (acc / l).astype(o_ref.dtype)


def _flash(q, kv, kdup, *, bsz, seq, cfg):
    n = q.shape[0]
    h = cfg.n_heads
    tq, tk = min(cfg.tq, seq), min(cfg.tk, seq)
    nq = seq // tq
    dq = q.shape[1] // h
    dn, dv = cfg.qk_nope, cfg.v_head
    assert dn == LANES and dv == LANES and dq == 2 * LANES
    return pl.pallas_call(
        functools.partial(_flash_kernel, tk=tk, nk=seq // tk),
        grid=(bsz, h, nq),
        in_specs=[
            pl.BlockSpec((tq, dq), lambda b, hh, i: (b * nq + i, hh)),
            pl.BlockSpec((seq, dn), lambda b, hh, i: (b, hh)),
            pl.BlockSpec((seq, LANES), lambda b, hh, i: (b, 0)),
            pl.BlockSpec((seq, dv), lambda b, hh, i: (b, h + hh)),
        ],
        out_specs=pl.BlockSpec((tq, dv), lambda b, hh, i: (b * nq + i, hh)),
        out_shape=jax.ShapeDtypeStruct((n, h * dv), BF16),
        compiler_params=_cp(("parallel", "parallel", "arbitrary"), 48),
    )(q, kv, kdup, kv)


def _out_kernel(x_ref, m_ref, w_ref, nw_ref, h_ref, hn_ref, *, eps):
    hh = x_ref[...] + jnp.dot(m_ref[...], w_ref[...], preferred_element_type=F32)
    h_ref[...] = hh
    ms = jnp.mean(hh * hh, axis=-1, keepdims=True)
    hn_ref[...] = (hh * lax.rsqrt(ms + eps) * nw_ref[...]).astype(hn_ref.dtype)


def _out_proj(x, merged, w_out, norm_w, *, cfg):
    n, dm = x.shape
    tm = cfg.tm_out
    return pl.pallas_call(
        functools.partial(_out_kernel, eps=cfg.eps),
        grid=(n // tm,),
        in_specs=[
            pl.BlockSpec((tm, dm), lambda i: (i, 0)),
            pl.BlockSpec((tm, dm), lambda i: (i, 0)),
            pl.BlockSpec((dm, dm), lambda i: (0, 0)),
            pl.BlockSpec((1, dm), lambda i: (0, 0)),
        ],
        out_specs=[pl.BlockSpec((tm, dm), lambda i: (i, 0)),
                   pl.BlockSpec((tm, dm), lambda i: (i, 0))],
        out_shape=[jax.ShapeDtypeStruct((n, dm), F32), jax.ShapeDtypeStruct((n, dm), BF16)],
        compiler_params=_cp(("parallel",), 48),
    )(x, merged, w_out, norm_w)


def _router_kernel(hn_ref, wt_ref, b_ref, idx_ref, gate_ref, rank_ref, cnt_ref, base_ref, *,
                   ne, topk):
    @pl.when(pl.program_id(0) == 0)
    def _():
        base_ref[...] = jnp.zeros_like(base_ref)

    tm = hn_ref.shape[0]
    logits = lax.dot_general(wt_ref[...], hn_ref[...], (((1,), (1,)), ((), ())),
                             preferred_element_type=F32) + b_ref[...]
    eio = lax.broadcasted_iota(jnp.int32, (ne, tm), 0)
    work = logits
    vals, onehots = [], []
    for k in range(topk):
        mx = jnp.max(work, axis=0, keepdims=True)
        sel = jnp.min(jnp.where(work == mx, eio, ne), axis=0, keepdims=True)
        oh = eio == sel
        vals.append(mx)
        onehots.append(oh)
        idx_ref[k:k + 1, :] = sel
        work = jnp.where(oh, -jnp.inf, work)
    es = [jnp.exp(v - vals[0]) for v in vals]
    den = es[0]
    for e in es[1:]:
        den = den + e
    for k in range(topk):
        gate_ref[k:k + 1, :] = es[k] / den
    a = onehots[0].astype(F32)
    for oh in onehots[1:]:
        a = a + oh.astype(F32)
    r = lax.broadcasted_iota(jnp.int32, (tm, tm), 0)
    c = lax.broadcasted_iota(jnp.int32, (tm, tm), 1)
    su = (r < c).astype(BF16)
    cum = jnp.dot(a.astype(BF16), su, preferred_element_type=F32) + base_ref[:, 0:1]
    for k in range(topk):
        rk = jnp.sum(jnp.where(onehots[k], cum, 0.0), axis=0, keepdims=True)
        rank_ref[k:k + 1, :] = rk.astype(jnp.int32)
    base_ref[...] = base_ref[...] + jnp.sum(a, axis=1, keepdims=True)
    cnt_ref[...] = base_ref[...].astype(jnp.int32)


def _router(hn, w_router_t, b_router, *, cfg):
    n, dm = hn.shape
    ne, topk = cfg.n_experts, cfg.top_k
    tm = cfg.tm_route
    assert n % tm == 0
    return pl.pallas_call(
        functools.partial(_router_kernel, ne=ne, topk=topk),
        grid=(n // tm,),
        in_specs=[
            pl.BlockSpec((tm, dm), lambda i: (i, 0)),
            pl.BlockSpec((ne, dm), lambda i: (0, 0)),
            pl.BlockSpec((ne, 1), lambda i: (0, 0)),
        ],
        out_specs=[
            pl.BlockSpec((topk, tm), lambda i: (0, i)),
            pl.BlockSpec((topk, tm), lambda i: (0, i)),
            pl.BlockSpec((topk, tm), lambda i: (0, i)),
            pl.BlockSpec((ne, LANES), lambda i: (0, 0)),
        ],
        out_shape=[
            jax.ShapeDtypeStruct((topk, n), jnp.int32),
            jax.ShapeDtypeStruct((topk, n), F32),
            jax.ShapeDtypeStruct((topk, n), jnp.int32),
            jax.ShapeDtypeStruct((ne, LANES), jnp.int32),
        ],
        scratch_shapes=[pltpu.VMEM((ne, LANES), F32)],
        compiler_params=_cp(("arbitrary",)),
    )(hn, w_router_t, b_router)


def _expert_kernel(be_ref, nb_ref, x_ref, wg_ref, wu_ref, bg_ref, bu_ref, wd_ref, bd_ref,
                   o_ref, acc_ref, *, limit, alpha, nj):
    i = pl.program_id(0)
    j = pl.program_id(1)

    @pl.when(i < nb_ref[0])
    def _():
        x = x_ref[...]
        g = jnp.dot(x, wg_ref[...], preferred_element_type=F32) + bg_ref[...]
        u = jnp.dot(x, wu_ref[...], preferred_element_type=F32) + bu_ref[...]
        g = jnp.minimum(g, limit)
        u = jnp.clip(u, -limit, limit)
        hh = (g * jax.nn.sigmoid(alpha * g) * (u + 1.0)).astype(BF16)
        part = jnp.dot(hh, wd_ref[...], preferred_element_type=F32)

        @pl.when(j == 0)
        def _():
            acc_ref[...] = part + bd_ref[...]

        @pl.when(j > 0)
        def _():
            acc_ref[...] = acc_ref[...] + part

    @pl.when(j == nj - 1)
    def _():
        o_ref[...] = acc_ref[...].astype(o_ref.dtype)


def _experts(xb, blk_e, n_used, w_gu, b_gu, w_down, b_down, *, cfg):
    rows, dm = xb.shape
    dff = cfg.d_ff
    tm, tf = cfg.tm_moe, min(cfg.tf_moe, dff)
    nj = dff // tf
    nblk = rows // tm
    grid_spec = pltpu.PrefetchScalarGridSpec(
        num_scalar_prefetch=2,
        grid=(nblk, nj),
        in_specs=[
            pl.BlockSpec((tm, dm), lambda i, j, be, nb: (i, 0)),
            pl.BlockSpec((None, dm, tf), lambda i, j, be, nb: (be[i], 0, j)),
            pl.BlockSpec((None, dm, tf), lambda i, j, be, nb: (be[i], 0, nj + j)),
            pl.BlockSpec((None, 1, tf), lambda i, j, be, nb: (be[i], 0, j)),
            pl.BlockSpec((None, 1, tf), lambda i, j, be, nb: (be[i], 0, nj + j)),
            pl.BlockSpec((None, tf, dm), lambda i, j, be, nb: (be[i], j, 0)),
            pl.BlockSpec((None, 1, dm), lambda i, j, be, nb: (be[i], 0, 0)),
        ],
        out_specs=pl.BlockSpec((tm, dm), lambda i, j, be, nb: (i, 0)),
        scratch_shapes=[pltpu.VMEM((tm, dm), F32)],
    )
    return pl.pallas_call(
        functools.partial(_expert_kernel, limit=cfg.limit, alpha=cfg.alpha, nj=nj),
        grid_spec=grid_spec,
        out_shape=jax.ShapeDtypeStruct((rows, dm), BF16),
        compiler_params=_cp(("arbitrary", "arbitrary"), 48),
    )(blk_e, n_used, xb, w_gu, w_gu, b_gu, b_gu, w_down, b_down)


def _combine_kernel(h_ref, y_ref, g_ref, nw_ref, o_ref, *, eps, topk):
    hh = h_ref[...]
    g = g_ref[...]
    for k in range(topk):
        hh = hh + g[:, k:k + 1] * y_ref[k].astype(F32)
    ms = jnp.mean(hh * hh, axis=-1, keepdims=True)
    o_ref[...] = hh * lax.rsqrt(ms + eps) * nw_ref[...]


def _combine(h, yk, gates_nk, norm_w, *, cfg):
    n, dm = h.shape
    topk = cfg.top_k
    tm = cfg.tm_comb
    return pl.pallas_call(
        functools.partial(_combine_kernel, eps=cfg.eps, topk=topk),
        grid=(n // tm,),
        in_specs=[
            pl.BlockSpec((tm, dm), lambda i: (i, 0)),
            pl.BlockSpec((topk, tm, dm), lambda i: (0, i, 0)),
            pl.BlockSpec((tm, topk), lambda i: (i, 0)),
            pl.BlockSpec((1, dm), lambda i: (0, 0)),
        ],
        out_specs=pl.BlockSpec((tm, dm), lambda i: (i, 0)),
        out_shape=jax.ShapeDtypeStruct((n, dm), F32),
        compiler_params=_cp(("parallel",), 48),
    )(h, yk, gates_nk, norm_w)


def _rot_cols(w):
    half = w.shape[-1] // 2
    return jnp.concatenate([-w[..., half:], w[..., :half]], axis=-1)


def _prep(cfg, norm_mix_w, w_in, q_norm_w, kv_norm_w, w_uq, w_ukv, conv_w, conv_b, dt_bias_f,
          dt_bias_b, a_log_f, a_log_b, d_skip, ssm_norm_w, w_br_attn, w_br_ssm, w_out,
          norm_ffn_w, w_router, b_router, w_gu, b_gu, w_down, b_down, norm_final_w):
    c = cfg
    hs = c.d_inner // c.ssm_head_dim
    conv_dim = c.d_inner + 2 * c.n_groups * c.d_state
    off_kv = c.q_lora
    off_kr = off_kv + c.kv_lora
    off_z = off_kr + c.qk_rope
    off_xbc = off_z + c.d_inner
    off_dtf = off_xbc + conv_dim
    off_dtb = off_dtf + hs
    off_ga = off_dtb + hs
    off_gs = off_ga + c.d_model
    w = w_in
    w_kr = w[:, off_kr:off_z]
    w_a = jnp.concatenate([w[:, :off_kr], w_kr, _rot_cols(w_kr), w[:, off_dtf:off_ga]], axis=1)
    w_b = jnp.concatenate([w[:, off_z:off_dtf], w[:, off_ga:]], axis=1)
    h = c.n_heads
    qk = c.qk_nope + c.qk_rope
    wq = w_uq.reshape(c.q_lora, h, qk)
    wq_r = wq[..., c.qk_nope:]
    wq2 = jnp.concatenate([wq[..., :c.qk_nope], wq_r, _rot_cols(wq_r)], axis=-1)
    wq2 = wq2.reshape(c.q_lora, h * (c.qk_nope + 2 * c.qk_rope))
    wkv = w_ukv.reshape(c.kv_lora, h, c.qk_nope + c.v_head)
    wkv2 = jnp.concatenate([wkv[..., :c.qk_nope].reshape(c.kv_lora, h * c.qk_nope),
                            wkv[..., c.qk_nope:].reshape(c.kv_lora, h * c.v_head)], axis=1)
    return dict(
        norm_mix_w=norm_mix_w, w_a=w_a.astype(BF16), w_b=w_b.astype(BF16),
        q_norm_w=q_norm_w, kv_norm_w=kv_norm_w, wq2=wq2.astype(BF16), wkv2=wkv2.astype(BF16),
        conv_w=conv_w, conv_b=conv_b,
        dt_bias2=jnp.concatenate([dt_bias_f, dt_bias_b]).reshape(1, 2 * hs).astype(F32),
        a2=(-jnp.exp(jnp.concatenate([a_log_f, a_log_b]).astype(F32))).reshape(1, 2 * hs),
        dskip_x=jnp.repeat(d_skip.astype(F32), c.ssm_head_dim).reshape(1, c.d_inner),
        ssm_norm_w=ssm_norm_w.reshape(1, c.d_inner).astype(F32),
        w_br_attn=w_br_attn.astype(BF16), w_br_ssm=w_br_ssm.astype(BF16),
        w_out=w_out.astype(BF16), norm_ffn_w=norm_ffn_w.reshape(1, c.d_model).astype(F32),
        w_router_t=w_router.T.astype(BF16), b_router=b_router.reshape(c.n_experts, 1).astype(F32),
        w_gu=w_gu.astype(BF16), b_gu=b_gu.reshape(c.n_experts, 1, 2 * c.d_ff).astype(F32),
        w_down=w_down.astype(BF16), b_down=b_down.reshape(c.n_experts, 1, c.d_model).astype(F32),
        norm_final_w=norm_final_w.reshape(1, c.d_model).astype(F32),
    )


def _rope_tabs(cfg, seq):
    half = cfg.qk_rope // 2
    inv_freq = 1.0 / (cfg.rope_theta ** (jnp.arange(0, cfg.qk_rope, 2, dtype=F32) / cfg.qk_rope))
    ang = jnp.arange(seq, dtype=F32)[:, None] * inv_freq[None, :]
    ang = jnp.concatenate([ang, ang], axis=-1)
    cos, sin = jnp.cos(ang), jnp.sin(ang)
    del half
    cs = jnp.concatenate([cos, sin], axis=1)
    qscale = (cfg.qk_nope + cfg.qk_rope) ** -0.5 * math.log2(math.e)
    qtab = jnp.concatenate([jnp.ones((seq, cfg.qk_nope), F32), cs], axis=1) * qscale
    return cs, qtab


def _moe_plan(idx, rank, counts, *, n, cfg):
    ne, topk, tm = cfg.n_experts, cfg.top_k, cfg.tm_moe
    nk = n * topk
    nblk = -(-(nk + ne * (tm - 1)) // tm)
    rows = nblk * tm
    padded = (counts + tm - 1) // tm * tm
    pad_ends = jnp.cumsum(padded)
    pad_starts = pad_ends - padded
    dest = pad_starts[idx] + rank
    tok = jnp.broadcast_to(jnp.arange(n, dtype=jnp.int32)[None, :], (topk, n))
    buf_tok = jnp.zeros((rows,), jnp.int32).at[dest.reshape(-1)].set(tok.reshape(-1))
    blk_start = jnp.arange(nblk, dtype=jnp.int32) * tm
    blk_e = jnp.minimum(jnp.searchsorted(pad_ends, blk_start, side="right"),
                        ne - 1).astype(jnp.int32)
    n_used = (pad_ends[-1] // tm).astype(jnp.int32).reshape(1)
    return dest, buf_tok, blk_e, n_used


def _trunk(x, p, cfg):
    c = cfg
    bsz, seq, dm = x.shape
    n = bsz * seq
    xf = x.reshape(n, dm)
    hs = c.d_inner // c.ssm_head_dim
    conv_dim = c.d_inner + 2 * c.n_groups * c.d_state
    tm = min(c.tm, seq)

    wa_cols = p["w_a"].shape[1]
    proj_a = _rms_matmul(xf, 0, dm, p["norm_mix_w"], p["w_a"], F32, tm=tm, tn=wa_cols, eps=c.eps)
    wb_cols = p["w_b"].shape[1]
    tn_b = 1024 if wb_cols % 1024 == 0 else 512
    proj_b = _rms_matmul(xf, 0, dm, p["norm_mix_w"], p["w_b"], BF16, tm=tm, tn=tn_b, eps=c.eps)
    col_kr = (c.q_lora + c.kv_lora) // LANES
    col_dt = col_kr + 1

    xbc = _conv_silu(proj_b, c.d_inner, p["conv_w"], p["conv_b"], bsz=bsz, seq=seq, cfg=c)
    y_f = _ssd_scan(xbc, proj_a, col_dt, p["dt_bias2"], p["a2"], reverse=False,
                    bsz=bsz, seq=seq, cfg=c)
    y_b = _ssd_scan(xbc, proj_a, col_dt, p["dt_bias2"], p["a2"], reverse=True,
                    bsz=bsz, seq=seq, cfg=c)

    cs, qtab = _rope_tabs(c, seq)
    assert c.q_lora == c.kv_lora
    q = _rms_matmul(proj_a, 0, c.q_lora, p["q_norm_w"], p["wq2"], BF16, tm=tm,
                    tn=min(1024, p["wq2"].shape[1]), eps=c.eps, tab=qtab, seq=seq)
    kv = _rms_matmul(proj_a, 1, c.kv_lora, p["kv_norm_w"], p["wkv2"], BF16, tm=tm,
                     tn=min(1024, p["wkv2"].shape[1]), eps=c.eps)
    kdup = _kdup(proj_a, col_kr, cs, seq=seq, tm=tm)
    attn = _flash(q, kv, kdup, bsz=bsz, seq=seq, cfg=c)
    ga_col0 = c.d_inner + conv_dim
    oa = _gated_matmul(attn, p["w_br_attn"], proj_b, ga_col0, tm=tm, tn=min(512, dm))

    merged = _ssm_out(y_f, y_b, xbc, proj_b, ga_col0 + dm, p["dskip_x"], p["ssm_norm_w"],
                      p["w_br_ssm"], oa, cfg=c)
    h, hn = _out_proj(xf, merged, p["w_out"], p["norm_ffn_w"], cfg=c)

    idx, gates, rank, cnt = _router(hn, p["w_router_t"], p["b_router"], cfg=c)
    dest, buf_tok, blk_e, n_used = _moe_plan(idx, rank, cnt[:, 0], n=n, cfg=c)
    xb = jnp.take(hn, buf_tok, axis=0)
    yb = _experts(xb, blk_e, n_used, p["w_gu"], p["b_gu"], p["w_down"], p["b_down"], cfg=c)
    yk = jnp.take(yb, dest.reshape(-1), axis=0).reshape(c.top_k, n, dm)
    out = _combine(h, yk, gates.T, p["norm_final_w"], cfg=c)
    del hs
    return out.reshape(bsz, seq, dm)


def _forward(cfg, x_prompt, x_sample, *weights):
    depth = weights[0].shape[0]
    assert depth == 1
    names_per_layer = [w[0] for w in weights[:-1]]
    p = _prep(cfg, *names_per_layer, weights[-1])
    return _trunk(x_prompt, p, cfg), _trunk(x_sample, p, cfg)


def kernel(x_prompt, x_sample, norm_mix_w, w_in, q_norm_w, kv_norm_w, w_uq, w_ukv, conv_w, conv_b,
           dt_bias_f, dt_bias_b, a_log_f, a_log_b, d_skip, ssm_norm_w, w_br_attn, w_br_ssm,
           w_out, norm_ffn_w, w_router, b_router, w_gu, b_gu, w_down, b_down, norm_final_w):
    return _forward(Cfg(), x_prompt, x_sample, norm_mix_w, w_in, q_norm_w, kv_norm_w, w_uq, w_ukv,
                    conv_w, conv_b, dt_bias_f, dt_bias_b, a_log_f, a_log_b, d_skip, ssm_norm_w,
                    w_br_attn, w_br_ssm, w_out, norm_ffn_w, w_router, b_router, w_gu, b_gu,
                    w_down, b_down, norm_final_w)
```

```python
import functools
import math
from typing import NamedTuple

import jax
import jax.numpy as jnp
from jax import lax
from jax.experimental import pallas as pl
from jax.experimental.pallas import tpu as pltpu

F32 = jnp.float32
BF16 = jnp.bfloat16
LANES = 128
HALF = 64


class Cfg(NamedTuple):
    d_model: int = 2048
    n_heads: int = 16
    q_lora: int = 512
    kv_lora: int = 512
    qk_nope: int = 128
    qk_rope: int = 64
    v_head: int = 128
    rope_theta: float = 10000.0
    d_inner: int = 4096
    ssm_head_dim: int = 64
    n_groups: int = 8
    d_state: int = 128
    conv_k: int = 5
    chunk: int = 128
    n_experts: int = 32
    top_k: int = 4
    d_ff: int = 2048
    limit: float = 7.0
    alpha: float = 1.702
    eps: float = 1e-6
    tm: int = 512
    tq: int = 512
    tk: int = 512
    flash_unroll: int = 4
    t_conv: int = 512
    tc_conv: int = 512
    tm_gate: int = 256
    tm_out: int = 256
    tm_route: int = 512
    tm_moe: int = 512
    tf_moe: int = 512
    tm_comb: int = 256


def _cp(sem, vmem_mb=None):
    kw = dict(dimension_semantics=sem)
    if vmem_mb is not None:
        kw["vmem_limit_bytes"] = vmem_mb * 1024 * 1024
    return pltpu.CompilerParams(**kw)


def _rms_mm_kernel(x_ref, nw_ref, w_ref, *rest, eps, tab_reps):
    if tab_reps:
        tab_ref, o_ref, xn_ref = rest
    else:
        o_ref, xn_ref = rest

    @pl.when(pl.program_id(1) == 0)
    def _():
        x = x_ref[...].astype(F32)
        ms = jnp.mean(x * x, axis=-1, keepdims=True)
        xn_ref[...] = (x * lax.rsqrt(ms + eps) * nw_ref[...]).astype(BF16)

    acc = jnp.dot(xn_ref[...], w_ref[...], preferred_element_type=F32)
    if tab_reps:
        t = tab_ref[...]
        if tab_reps > 1:
            t = jnp.concatenate([t] * tab_reps, axis=1)
        acc = acc * t
    o_ref[...] = acc.astype(o_ref.dtype)


def _rms_matmul(x, x_colblk, k, norm_w, w, out_dtype, *, tm, tn, eps, name, tab=None, seq=None):
    n = x.shape[0]
    nout = w.shape[1]
    assert n % tm == 0 and nout % tn == 0 and w.shape[0] == k
    in_specs = [
        pl.BlockSpec((tm, k), lambda i, j: (i, x_colblk)),
        pl.BlockSpec((1, k), lambda i, j: (0, 0)),
        pl.BlockSpec((k, tn), lambda i, j: (0, j)),
    ]
    args = [x, norm_w.reshape(1, k).astype(F32), w]
    tab_reps = 0
    if tab is not None:
        tw = tab.shape[1]
        assert tn % tw == 0 and seq % tm == 0
        tab_reps = tn // tw
        nst = seq // tm
        in_specs.append(pl.BlockSpec((tm, tw), lambda i, j: (i % nst, 0)))
        args.append(tab)
    return pl.pallas_call(
        functools.partial(_rms_mm_kernel, eps=eps, tab_reps=tab_reps),
        grid=(n // tm, nout // tn),
        in_specs=in_specs,
        out_specs=pl.BlockSpec((tm, tn), lambda i, j: (i, j)),
        out_shape=jax.ShapeDtypeStruct((n, nout), out_dtype),
        scratch_shapes=[pltpu.VMEM((tm, k), BF16)],
        compiler_params=_cp(("parallel", "arbitrary"), 48),
        name=name,
    )(*args)


def _rms_mm_t_kernel(x_ref, nw_ref, wt_ref, *rest, eps, tab_reps):
    if tab_reps:
        tab_ref, o_ref, xn_ref = rest
    else:
        o_ref, xn_ref = rest

    @pl.when(pl.program_id(1) == 0)
    def _():
        x = x_ref[...].astype(F32)
        ms = jnp.mean(x * x, axis=-1, keepdims=True)
        xn_ref[...] = (x * lax.rsqrt(ms + eps) * nw_ref[...]).astype(BF16)

    acc = lax.dot_general(wt_ref[...], xn_ref[...], (((1,), (1,)), ((), ())),
                          preferred_element_type=F32)
    if tab_reps:
        t = tab_ref[...]
        if tab_reps > 1:
            t = jnp.concatenate([t] * tab_reps, axis=0)
        acc = acc * t
    o_ref[...] = acc.astype(o_ref.dtype)


def _rms_matmul_t(x, x_colblk, k, norm_w, wt, out_dtype, *, tm, tn, eps, name, tab_t=None,
                  seq=None):
    n = x.shape[0]
    nout = wt.shape[0]
    assert n % tm == 0 and nout % tn == 0 and wt.shape[1] == k
    in_specs = [
        pl.BlockSpec((tm, k), lambda i, j: (i, x_colblk)),
        pl.BlockSpec((1, k), lambda i, j: (0, 0)),
        pl.BlockSpec((tn, k), lambda i, j: (j, 0)),
    ]
    args = [x, norm_w.reshape(1, k).astype(F32), wt]
    tab_reps = 0
    if tab_t is not None:
        tw = tab_t.shape[0]
        assert tn % tw == 0 and seq % tm == 0
        tab_reps = tn // tw
        nst = seq // tm
        in_specs.append(pl.BlockSpec((tw, tm), lambda i, j: (0, i % nst)))
        args.append(tab_t)
    return pl.pallas_call(
        functools.partial(_rms_mm_t_kernel, eps=eps, tab_reps=tab_reps),
        grid=(n // tm, nout // tn),
        in_specs=in_specs,
        out_specs=pl.BlockSpec((tn, tm), lambda i, j: (j, i)),
        out_shape=jax.ShapeDtypeStruct((nout, n), out_dtype),
        scratch_shapes=[pltpu.VMEM((tm, k), BF16)],
        compiler_params=_cp(("parallel", "arbitrary"), 48),
        name=name,
    )(*args)


HALO = 16


def _conv_kernel(prev_ref, x_ref, next_ref, w_ref, b_ref, o_ref, buf_ref, *, t, nt, kk):
    ti = pl.program_id(1)
    pv = prev_ref[...].astype(F32)[HALO - 8:, :]
    nx = next_ref[...].astype(F32)[:8, :]
    pv = jnp.where(ti == 0, 0.0, pv)
    nx = jnp.where(ti == nt - 1, 0.0, nx)
    buf_ref[0:8, :] = pv
    buf_ref[8:t + 8, :] = x_ref[...].astype(F32)
    buf_ref[t + 8:t + 16, :] = nx
    acc = b_ref[...]
    half = kk // 2
    for k in range(kk):
        acc = acc + w_ref[k:k + 1, :] * buf_ref[pl.ds(8 - half + k, t), :]
    o_ref[...] = (acc * jax.nn.sigmoid(acc)).astype(o_ref.dtype)


def _conv_silu(proj_b, col0, conv_w, conv_b, *, bsz, seq, cfg):
    n = proj_b.shape[0]
    cdim = conv_w.shape[1]
    t, tc = min(cfg.t_conv, seq), cfg.tc_conv
    assert seq % t == 0 and cdim % tc == 0 and col0 % tc == 0 and t % HALO == 0
    nt = seq // t
    cb0 = col0 // tc
    nrb = n // HALO
    rb = t // HALO

    def prev_map(b, ti, j):
        return (jnp.maximum((b * nt + ti) * rb - 1, 0), cb0 + j)

    def next_map(b, ti, j):
        return (jnp.minimum((b * nt + ti + 1) * rb, nrb - 1), cb0 + j)

    return pl.pallas_call(
        functools.partial(_conv_kernel, t=t, nt=nt, kk=cfg.conv_k),
        grid=(bsz, nt, cdim // tc),
        in_specs=[
            pl.BlockSpec((HALO, tc), prev_map),
            pl.BlockSpec((t, tc), lambda b, ti, j: (b * nt + ti, cb0 + j)),
            pl.BlockSpec((HALO, tc), next_map),
            pl.BlockSpec((cfg.conv_k, tc), lambda b, ti, j: (0, j)),
            pl.BlockSpec((1, tc), lambda b, ti, j: (0, j)),
        ],
        out_specs=pl.BlockSpec((t, tc), lambda b, ti, j: (b * nt + ti, j)),
        out_shape=jax.ShapeDtypeStruct((n, cdim), BF16),
        scratch_shapes=[pltpu.VMEM((t + 16, tc), F32)],
        compiler_params=_cp(("parallel", "parallel", "parallel")),
        name="conv_silu",
    )(proj_b, proj_b, proj_b, conv_w.astype(F32), conv_b.reshape(1, cdim).astype(F32))


def _ssd_kernel(xs_ref, b_ref, c_ref, dt_ref, bias_ref, a_ref, y_ref, st_ref, *,
                reverse, ll, ng, rr, pp, nn):
    assert pp == HALF and rr % 2 == 0 and 2 * ng * rr == LANES
    ci = pl.program_id(1)

    @pl.when(ci == 0)
    def _():
        st_ref[...] = jnp.zeros_like(st_ref)

    dcol = ng * rr if reverse else 0
    z = dt_ref[...] + bias_ref[...]
    dt_all = jnp.maximum(z, 0.0) + jnp.log(1.0 + jnp.exp(-jnp.abs(z)))
    dta = dt_all * a_ref[...]
    row = lax.broadcasted_iota(jnp.int32, (ll, ll), 0)
    col = lax.broadcasted_iota(jnp.int32, (ll, ll), 1)
    mask = (row <= col) if reverse else (row >= col)
    tri = mask.astype(F32)
    acum = jnp.dot(tri, dta, preferred_element_type=F32, precision=lax.Precision.HIGHEST)
    total = acum[0:1, :] if reverse else acum[ll - 1:ll, :]
    eac = jnp.exp(acum)
    wend = jnp.exp(total - acum) * dt_all
    etot = jnp.exp(total)
    acum_t = acum.T
    dt_t = dt_all.T
    lane = lax.broadcasted_iota(jnp.int32, (ll, LANES), 1)
    lo = lane < HALF
    lane1 = lax.broadcasted_iota(jnp.int32, (1, LANES), 1)
    lo1 = lane1 < HALF

    for g in range(ng):
        bg = b_ref[:, g * nn:(g + 1) * nn]
        cg = c_ref[:, g * nn:(g + 1) * nn]
        cb = lax.dot_general(cg, bg, (((1,), (1,)), ((), ())), preferred_element_type=F32)
        st = st_ref[g]
        gw = rr * pp
        yint = jnp.dot(cg, st.astype(BF16), preferred_element_type=F32)
        xw_parts = []
        et_parts = []
        for pr in range(rr // 2):
            c0 = dcol + g * rr + 2 * pr
            lanes0 = g * gw + pr * LANES
            xp = xs_ref[:, lanes0:lanes0 + LANES]
            ys = []
            for c in (c0, c0 + 1):
                seg = acum[:, c:c + 1] - acum_t[c:c + 1, :]
                dec = jnp.exp(jnp.where(mask, seg, -jnp.inf))
                m = (cb * dec * dt_t[c:c + 1, :]).astype(BF16)
                ys.append(jnp.dot(m, xp, preferred_element_type=F32))
            e_sel = jnp.where(lo, eac[:, c0:c0 + 1], eac[:, c0 + 1:c0 + 2])
            y = jnp.where(lo, ys[0], ys[1]) + yint[:, pr * LANES:(pr + 1) * LANES] * e_sel
            y_ref[:, lanes0:lanes0 + LANES] = y.astype(y_ref.dtype)
            w_sel = jnp.where(lo, wend[:, c0:c0 + 1], wend[:, c0 + 1:c0 + 2])
            xw_parts.append((xp.astype(F32) * w_sel).astype(BF16))
            et_parts.append(jnp.where(lo1, etot[:, c0:c0 + 1], etot[:, c0 + 1:c0 + 2]))
        xw = jnp.concatenate(xw_parts, axis=1)
        et = jnp.concatenate(et_parts, axis=1)
        upd = lax.dot_general(bg, xw, (((0,), (0,)), ((), ())), preferred_element_type=F32)
        st_ref[g] = st * et + upd


def _ssd_scan(xbc, dt_src, dt_colblk, bias2, a2, *, reverse, bsz, seq, cfg):
    n = xbc.shape[0]
    ll, ng, pp, nn = cfg.chunk, cfg.n_groups, cfg.ssm_head_dim, cfg.d_state
    di = cfg.d_inner
    rr = di // pp // ng
    nc = seq // ll
    gn = ng * nn
    assert di % gn == 0
    boff = di // gn

    def rblk(b, c):
        return b * nc + ((nc - 1 - c) if reverse else c)

    return pl.pallas_call(
        functools.partial(_ssd_kernel, reverse=reverse, ll=ll, ng=ng, rr=rr, pp=pp, nn=nn),
        grid=(bsz, nc),
        in_specs=[
            pl.BlockSpec((ll, di), lambda b, c: (rblk(b, c), 0)),
            pl.BlockSpec((ll, gn), lambda b, c: (rblk(b, c), boff)),
            pl.BlockSpec((ll, gn), lambda b, c: (rblk(b, c), boff + 1)),
            pl.BlockSpec((ll, LANES), lambda b, c: (rblk(b, c), dt_colblk)),
            pl.BlockSpec((1, LANES), lambda b, c: (0, 0)),
            pl.BlockSpec((1, LANES), lambda b, c: (0, 0)),
        ],
        out_specs=pl.BlockSpec((ll, di), lambda b, c: (rblk(b, c), 0)),
        out_shape=jax.ShapeDtypeStruct((n, di), BF16),
        scratch_shapes=[pltpu.VMEM((ng, nn, rr * pp), F32)],
        compiler_params=_cp(("parallel", "arbitrary")),
        name="ssd_bwd" if reverse else "ssd_fwd",
    )(xbc, xbc, xbc, dt_src, bias2, a2)


def _ssm_out_kernel(yf_ref, yb_ref, xs_ref, z_ref, dsk_ref, nw_ref, w_ref, gs_ref, oa_ref,
                    o_ref, yn_ref, *, eps, ng):
    @pl.when(pl.program_id(1) == 0)
    def _():
        di = yn_ref.shape[1]
        gw = di // ng
        for g in range(ng):
            sl = slice(g * gw, (g + 1) * gw)
            y = (yf_ref[:, sl].astype(F32) + yb_ref[:, sl].astype(F32)
                 + dsk_ref[:, sl] * xs_ref[:, sl].astype(F32))
            zz = z_ref[:, sl].astype(F32)
            y = y * (zz * jax.nn.sigmoid(zz))
            ms = jnp.mean(y * y, axis=-1, keepdims=True)
            yn_ref[:, sl] = (y * lax.rsqrt(ms + eps) * nw_ref[:, sl]).astype(BF16)

    acc = jnp.dot(yn_ref[...], w_ref[...], preferred_element_type=F32)
    gate = jax.nn.sigmoid(gs_ref[...].astype(F32))
    o_ref[...] = (oa_ref[...].astype(F32) + gate * acc).astype(o_ref.dtype)


def _ssm_out(yf, yb, xbc, proj_b, gs_col0, dskip_x, norm_w, w_br, oa, *, cfg):
    n = yf.shape[0]
    di, dm = cfg.d_inner, cfg.d_model
    tm, tn = cfg.tm_gate, min(512, dm)
    assert n % tm == 0 and dm % tn == 0 and gs_col0 % tn == 0
    gsb = gs_col0 // tn
    return pl.pallas_call(
        functools.partial(_ssm_out_kernel, eps=cfg.eps, ng=cfg.n_groups),
        grid=(n // tm, dm // tn),
        in_specs=[
            pl.BlockSpec((tm, di), lambda i, j: (i, 0)),
            pl.BlockSpec((tm, di), lambda i, j: (i, 0)),
            pl.BlockSpec((tm, di), lambda i, j: (i, 0)),
            pl.BlockSpec((tm, di), lambda i, j: (i, 0)),
            pl.BlockSpec((1, di), lambda i, j: (0, 0)),
            pl.BlockSpec((1, di), lambda i, j: (0, 0)),
            pl.BlockSpec((di, tn), lambda i, j: (0, j)),
            pl.BlockSpec((tm, tn), lambda i, j: (i, gsb + j)),
            pl.BlockSpec((tm, tn), lambda i, j: (i, j)),
        ],
        out_specs=pl.BlockSpec((tm, tn), lambda i, j: (i, j)),
        out_shape=jax.ShapeDtypeStruct((n, dm), BF16),
        scratch_shapes=[pltpu.VMEM((tm, di), BF16)],
        compiler_params=_cp(("parallel", "arbitrary"), 48),
        name="ssm_out",
    )(yf, yb, xbc, proj_b, dskip_x, norm_w, w_br, proj_b, oa)


def _gated_mm_kernel(x_ref, w_ref, g_ref, o_ref):
    acc = jnp.dot(x_ref[...], w_ref[...], preferred_element_type=F32)
    o_ref[...] = (jax.nn.sigmoid(g_ref[...].astype(F32)) * acc).astype(o_ref.dtype)


def _gated_matmul(x, w, gate_src, gate_col0, *, tm, tn):
    n, k = x.shape
    nout = w.shape[1]
    assert n % tm == 0 and nout % tn == 0 and gate_col0 % tn == 0
    gb = gate_col0 // tn
    return pl.pallas_call(
        _gated_mm_kernel,
        grid=(n // tm, nout // tn),
        in_specs=[
            pl.BlockSpec((tm, k), lambda i, j: (i, 0)),
            pl.BlockSpec((k, tn), lambda i, j: (0, j)),
            pl.BlockSpec((tm, tn), lambda i, j: (i, gb + j)),
        ],
        out_specs=pl.BlockSpec((tm, tn), lambda i, j: (i, j)),
        out_shape=jax.ShapeDtypeStruct((n, nout), BF16),
        compiler_params=_cp(("parallel", "arbitrary"), 48),
        name="attn_br",
    )(x, w, gate_src)


def _kdup_kernel(kr_ref, cs_ref, o_ref):
    k2 = kr_ref[...].astype(F32) * cs_ref[...]
    o_ref[...] = (k2 + pltpu.roll(k2, HALF, 1)).astype(o_ref.dtype)


def _kdup(proj_a, colblk, cs, *, seq, tm):
    n = proj_a.shape[0]
    tm = min(tm, seq)
    nst = seq // tm
    return pl.pallas_call(
        _kdup_kernel,
        grid=(n // tm,),
        in_specs=[
            pl.BlockSpec((tm, LANES), lambda i: (i, colblk)),
            pl.BlockSpec((tm, LANES), lambda i: (i % nst, 0)),
        ],
        out_specs=pl.BlockSpec((tm, LANES), lambda i: (i, 0)),
        out_shape=jax.ShapeDtypeStruct((n, LANES), BF16),
        compiler_params=_cp(("parallel",)),
        name="kdup",
    )(proj_a, cs)


ONES_ROWS = 16


def _flash_kernel(qt_ref, kn_ref, kd_ref, vt_ref, o_ref, s_ref, m_ref, acc_ref, *,
                  tk, nk, unroll):
    qt = qt_ref[...]

    def scores(c):
        off = pl.multiple_of(c * tk, tk)
        kcat = jnp.concatenate([kn_ref[pl.ds(off, tk), :], kd_ref[pl.ds(off, tk), :]], axis=1)
        return jnp.dot(kcat, qt, preferred_element_type=F32)

    dv = vt_ref.shape[0]
    ones = jnp.ones((ONES_ROWS, tk), BF16)

    def update(c, s):
        off = pl.multiple_of(c * tk, tk)
        m = m_ref[...]
        m_new = jnp.maximum(m, jnp.max(s, axis=0, keepdims=True))
        alpha = jnp.exp2(m - m_new)
        p = jnp.exp2((s - m_new).astype(BF16))
        m_ref[...] = m_new
        lhs = jnp.concatenate([vt_ref[:, pl.ds(off, tk)], ones], axis=0)
        acc_ref[...] = alpha * acc_ref[...] + jnp.dot(lhs, p, preferred_element_type=F32)

    m_ref[...] = jnp.full(m_ref.shape, -jnp.inf, F32)
    acc_ref[...] = jnp.zeros(acc_ref.shape, F32)
    s_ref[...] = scores(0)

    def trip(c, last):
        s_cur = s_ref[...]
        for u in range(unroll):
            s_next = None if (last and u == unroll - 1) else scores(c + u + 1)
            update(c + u, s_cur)
            s_cur = s_next
        if not last:
            s_ref[...] = s_cur

    def body(ci, carry):
        trip(ci * unroll, False)
        return carry

    lax.fori_loop(0, nk // unroll - 1, body, 0)
    trip(nk - unroll, True)
    o_ref[...] = (acc_ref[0:dv, :] / acc_ref[dv:dv + 1, :]).T.astype(o_ref.dtype)


def _flash(qt, kn, kdup, vt, *, bsz, seq, cfg):
    n = qt.shape[1]
    h = cfg.n_heads
    tq, tk = min(cfg.tq, seq), min(cfg.tk, seq)
    nq = seq // tq
    dq = qt.shape[0] // h
    dn, dv = cfg.qk_nope, cfg.v_head
    assert dn == LANES and dv == LANES and dq == 2 * LANES
    nk = seq // tk
    unroll = math.gcd(nk, cfg.flash_unroll)
    return pl.pallas_call(
        functools.partial(_flash_kernel, tk=tk, nk=nk, unroll=unroll),
        grid=(bsz, h, nq),
        in_specs=[
            pl.BlockSpec((dq, tq), lambda b, hh, i: (hh, b * nq + i)),
            pl.BlockSpec((seq, dn), lambda b, hh, i: (b, hh)),
            pl.BlockSpec((seq, LANES), lambda b, hh, i: (b, 0)),
            pl.BlockSpec((dv, seq), lambda b, hh, i: (hh, b)),
        ],
        out_specs=pl.BlockSpec((tq, dv), lambda b, hh, i: (b * nq + i, hh)),
        out_shape=jax.ShapeDtypeStruct((n, h * dv), BF16),
        scratch_shapes=[pltpu.VMEM((tk, tq), F32), pltpu.VMEM((1, tq), F32),
                        pltpu.VMEM((dv + ONES_ROWS, tq), F32)],
        compiler_params=_cp(("parallel", "parallel", "arbitrary"), 48),
        name="flash",
    )(qt, kn, kdup, vt)


def _out_kernel(x_ref, m_ref, w_ref, nw_ref, h_ref, hn_ref, *, eps):
    hh = x_ref[...] + jnp.dot(m_ref[...], w_ref[...], preferred_element_type=F32)
    h_ref[...] = hh
    ms = jnp.mean(hh * hh, axis=-1, keepdims=True)
    hn_ref[...] = (hh * lax.rsqrt(ms + eps) * nw_ref[...]).astype(hn_ref.dtype)


def _out_proj(x, merged, w_out, norm_w, *, cfg):
    n, dm = x.shape
    tm = cfg.tm_out
    return pl.pallas_call(
        functools.partial(_out_kernel, eps=cfg.eps),
        grid=(n // tm,),
        in_specs=[
            pl.BlockSpec((tm, dm), lambda i: (i, 0)),
            pl.BlockSpec((tm, dm), lambda i: (i, 0)),
            pl.BlockSpec((dm, dm), lambda i: (0, 0)),
            pl.BlockSpec((1, dm), lambda i: (0, 0)),
        ],
        out_specs=[pl.BlockSpec((tm, dm), lambda i: (i, 0)),
                   pl.BlockSpec((tm, dm), lambda i: (i, 0))],
        out_shape=[jax.ShapeDtypeStruct((n, dm), F32), jax.ShapeDtypeStruct((n, dm), BF16)],
        compiler_params=_cp(("parallel",), 48),
        name="out_proj",
    )(x, merged, w_out, norm_w)


def _router_kernel(hn_ref, wt_ref, b_ref, idx_ref, gate_ref, rank_ref, cnt_ref, base_ref, *,
                   ne, topk):
    @pl.when(pl.program_id(0) == 0)
    def _():
        base_ref[...] = jnp.zeros_like(base_ref)

    tm = hn_ref.shape[0]
    logits = lax.dot_general(wt_ref[...], hn_ref[...], (((1,), (1,)), ((), ())),
                             preferred_element_type=F32) + b_ref[...]
    eio = lax.broadcasted_iota(jnp.int32, (ne, tm), 0)
    work = logits
    vals, onehots = [], []
    for k in range(topk):
        mx = jnp.max(work, axis=0, keepdims=True)
        sel = jnp.min(jnp.where(work == mx, eio, ne), axis=0, keepdims=True)
        oh = eio == sel
        vals.append(mx)
        onehots.append(oh)
        idx_ref[k:k + 1, :] = sel
        work = jnp.where(oh, -jnp.inf, work)
    es = [jnp.exp(v - vals[0]) for v in vals]
    den = es[0]
    for e in es[1:]:
        den = den + e
    for k in range(topk):
        gate_ref[k:k + 1, :] = es[k] / den
    a = onehots[0].astype(F32)
    for oh in onehots[1:]:
        a = a + oh.astype(F32)
    r = lax.broadcasted_iota(jnp.int32, (tm, tm), 0)
    c = lax.broadcasted_iota(jnp.int32, (tm, tm), 1)
    su = (r < c).astype(BF16)
    cum = jnp.dot(a.astype(BF16), su, preferred_element_type=F32) + base_ref[:, 0:1]
    for k in range(topk):
        rk = jnp.sum(jnp.where(onehots[k], cum, 0.0), axis=0, keepdims=True)
        rank_ref[k:k + 1, :] = rk.astype(jnp.int32)
    base_ref[...] = base_ref[...] + jnp.sum(a, axis=1, keepdims=True)
    cnt_ref[...] = base_ref[...].astype(jnp.int32)


def _router(hn, w_router_t, b_router, *, cfg):
    n, dm = hn.shape
    ne, topk = cfg.n_experts, cfg.top_k
    tm = cfg.tm_route
    assert n % tm == 0
    return pl.pallas_call(
        functools.partial(_router_kernel, ne=ne, topk=topk),
        grid=(n // tm,),
        in_specs=[
            pl.BlockSpec((tm, dm), lambda i: (i, 0)),
            pl.BlockSpec((ne, dm), lambda i: (0, 0)),
            pl.BlockSpec((ne, 1), lambda i: (0, 0)),
        ],
        out_specs=[
            pl.BlockSpec((topk, tm), lambda i: (0, i)),
            pl.BlockSpec((topk, tm), lambda i: (0, i)),
            pl.BlockSpec((topk, tm), lambda i: (0, i)),
            pl.BlockSpec((ne, LANES), lambda i: (0, 0)),
        ],
        out_shape=[
            jax.ShapeDtypeStruct((topk, n), jnp.int32),
            jax.ShapeDtypeStruct((topk, n), F32),
            jax.ShapeDtypeStruct((topk, n), jnp.int32),
            jax.ShapeDtypeStruct((ne, LANES), jnp.int32),
        ],
        scratch_shapes=[pltpu.VMEM((ne, LANES), F32)],
        compiler_params=_cp(("arbitrary",)),
        name="router",
    )(hn, w_router_t, b_router)


def _expert_kernel(be_ref, nb_ref, x_ref, wg_ref, wu_ref, bg_ref, bu_ref, wd_ref, bd_ref,
                   o_ref, acc_ref, *, limit, alpha, nj):
    i = pl.program_id(0)
    j = pl.program_id(1)

    @pl.when(i < nb_ref[0])
    def _():
        x = x_ref[...]
        g = jnp.dot(x, wg_ref[...], preferred_element_type=F32) + bg_ref[...]
        u = jnp.dot(x, wu_ref[...], preferred_element_type=F32) + bu_ref[...]
        g = jnp.minimum(g, limit)
        u = jnp.clip(u, -limit, limit)
        hh = (g * jax.nn.sigmoid(alpha * g) * (u + 1.0)).astype(BF16)
        part = jnp.dot(hh, wd_ref[...], preferred_element_type=F32)

        @pl.when(j == 0)
        def _():
            acc_ref[...] = part + bd_ref[...]

        @pl.when(j > 0)
        def _():
            acc_ref[...] = acc_ref[...] + part

    @pl.when(j == nj - 1)
    def _():
        o_ref[...] = acc_ref[...].astype(o_ref.dtype)


def _experts(xb, blk_e, n_used, w_gu, b_gu, w_down, b_down, *, cfg):
    rows, dm = xb.shape
    dff = cfg.d_ff
    tm, tf = cfg.tm_moe, min(cfg.tf_moe, dff)
    nj = dff // tf
    nblk = rows // tm
    grid_spec = pltpu.PrefetchScalarGridSpec(
        num_scalar_prefetch=2,
        grid=(nblk, nj),
        in_specs=[
            pl.BlockSpec((tm, dm), lambda i, j, be, nb: (i, 0)),
            pl.BlockSpec((None, dm, tf), lambda i, j, be, nb: (be[i], 0, j)),
            pl.BlockSpec((None, dm, tf), lambda i, j, be, nb: (be[i], 0, nj + j)),
            pl.BlockSpec((None, 1, tf), lambda i, j, be, nb: (be[i], 0, j)),
            pl.BlockSpec((None, 1, tf), lambda i, j, be, nb: (be[i], 0, nj + j)),
            pl.BlockSpec((None, tf, dm), lambda i, j, be, nb: (be[i], j, 0)),
            pl.BlockSpec((None, 1, dm), lambda i, j, be, nb: (be[i], 0, 0)),
        ],
        out_specs=pl.BlockSpec((tm, dm), lambda i, j, be, nb: (i, 0)),
        scratch_shapes=[pltpu.VMEM((tm, dm), F32)],
    )
    return pl.pallas_call(
        functools.partial(_expert_kernel, limit=cfg.limit, alpha=cfg.alpha, nj=nj),
        grid_spec=grid_spec,
        out_shape=jax.ShapeDtypeStruct((rows, dm), BF16),
        compiler_params=_cp(("arbitrary", "arbitrary"), 48),
        name="experts",
    )(blk_e, n_used, xb, w_gu, w_gu, b_gu, b_gu, w_down, b_down)


def _combine_kernel(h_ref, y_ref, g_ref, nw_ref, o_ref, *, eps, topk):
    hh = h_ref[...]
    g = g_ref[...]
    for k in range(topk):
        hh = hh + g[:, k:k + 1] * y_ref[k].astype(F32)
    ms = jnp.mean(hh * hh, axis=-1, keepdims=True)
    o_ref[...] = hh * lax.rsqrt(ms + eps) * nw_ref[...]


def _combine(h, yk, gates_nk, norm_w, *, cfg):
    n, dm = h.shape
    topk = cfg.top_k
    tm = cfg.tm_comb
    return pl.pallas_call(
        functools.partial(_combine_kernel, eps=cfg.eps, topk=topk),
        grid=(n // tm,),
        in_specs=[
            pl.BlockSpec((tm, dm), lambda i: (i, 0)),
            pl.BlockSpec((topk, tm, dm), lambda i: (0, i, 0)),
            pl.BlockSpec((tm, topk), lambda i: (i, 0)),
            pl.BlockSpec((1, dm), lambda i: (0, 0)),
        ],
        out_specs=pl.BlockSpec((tm, dm), lambda i: (i, 0)),
        out_shape=jax.ShapeDtypeStruct((n, dm), F32),
        compiler_params=_cp(("parallel",), 48),
        name="combine",
    )(h, yk, gates_nk, norm_w)


def _cast_kernel(x_ref, o_ref):
    o_ref[...] = x_ref[...].astype(o_ref.dtype)


def _cast_bf16(w, *, block_bytes=8 * 1024 * 1024):
    e, r, c = w.shape
    tr = max(8, min(r, block_bytes // (c * 4)))
    assert r % tr == 0
    return pl.pallas_call(
        _cast_kernel,
        grid=(e, r // tr),
        in_specs=[pl.BlockSpec((None, tr, c), lambda i, j: (i, j, 0))],
        out_specs=pl.BlockSpec((None, tr, c), lambda i, j: (i, j, 0)),
        out_shape=jax.ShapeDtypeStruct(w.shape, BF16),
        compiler_params=_cp(("parallel", "parallel"), 40),
        name="cast_bf16",
    )(w)


def _rot_cols(w):
    half = w.shape[-1] // 2
    return jnp.concatenate([-w[..., half:], w[..., :half]], axis=-1)


def _prep(cfg, norm_mix_w, w_in, q_norm_w, kv_norm_w, w_uq, w_ukv, conv_w, conv_b, dt_bias_f,
          dt_bias_b, a_log_f, a_log_b, d_skip, ssm_norm_w, w_br_attn, w_br_ssm, w_out,
          norm_ffn_w, w_router, b_router, w_gu, b_gu, w_down, b_down, norm_final_w):
    c = cfg
    hs = c.d_inner // c.ssm_head_dim
    conv_dim = c.d_inner + 2 * c.n_groups * c.d_state
    off_kv = c.q_lora
    off_kr = off_kv + c.kv_lora
    off_z = off_kr + c.qk_rope
    off_xbc = off_z + c.d_inner
    off_dtf = off_xbc + conv_dim
    off_dtb = off_dtf + hs
    off_ga = off_dtb + hs
    off_gs = off_ga + c.d_model
    w = w_in
    w_kr = w[:, off_kr:off_z]
    w_a = jnp.concatenate([w[:, :off_kr], w_kr, _rot_cols(w_kr), w[:, off_dtf:off_ga]], axis=1)
    w_b = jnp.concatenate([w[:, off_z:off_dtf], w[:, off_ga:]], axis=1)
    h = c.n_heads
    qk = c.qk_nope + c.qk_rope
    wq = w_uq.reshape(c.q_lora, h, qk)
    wq_r = wq[..., c.qk_nope:]
    wq2 = jnp.concatenate([wq[..., :c.qk_nope], wq_r, _rot_cols(wq_r)], axis=-1)
    wq2 = wq2.reshape(c.q_lora, h * (c.qk_nope + 2 * c.qk_rope))
    wkv = w_ukv.reshape(c.kv_lora, h, c.qk_nope + c.v_head)
    wk = wkv[..., :c.qk_nope].reshape(c.kv_lora, h * c.qk_nope)
    wvt = wkv[..., c.qk_nope:].reshape(c.kv_lora, h * c.v_head).T
    return dict(
        norm_mix_w=norm_mix_w, w_a=w_a.astype(BF16), w_b=w_b.astype(BF16),
        q_norm_w=q_norm_w, kv_norm_w=kv_norm_w, wq2t=wq2.T.astype(BF16), wk=wk.astype(BF16),
        wvt=wvt.astype(BF16),
        conv_w=conv_w, conv_b=conv_b,
        dt_bias2=jnp.concatenate([dt_bias_f, dt_bias_b]).reshape(1, 2 * hs).astype(F32),
        a2=(-jnp.exp(jnp.concatenate([a_log_f, a_log_b]).astype(F32))).reshape(1, 2 * hs),
        dskip_x=jnp.repeat(d_skip.astype(F32), c.ssm_head_dim).reshape(1, c.d_inner),
        ssm_norm_w=ssm_norm_w.reshape(1, c.d_inner).astype(F32),
        w_br_attn=w_br_attn.astype(BF16), w_br_ssm=w_br_ssm.astype(BF16),
        w_out=w_out.astype(BF16), norm_ffn_w=norm_ffn_w.reshape(1, c.d_model).astype(F32),
        w_router_t=w_router.T.astype(BF16), b_router=b_router.reshape(c.n_experts, 1).astype(F32),
        w_gu=_cast_bf16(w_gu), b_gu=b_gu.reshape(c.n_experts, 1, 2 * c.d_ff).astype(F32),
        w_down=_cast_bf16(w_down), b_down=b_down.reshape(c.n_experts, 1, c.d_model).astype(F32),
        norm_final_w=norm_final_w.reshape(1, c.d_model).astype(F32),
    )


def _rope_tabs(cfg, seq):
    half = cfg.qk_rope // 2
    inv_freq = 1.0 / (cfg.rope_theta ** (jnp.arange(0, cfg.qk_rope, 2, dtype=F32) / cfg.qk_rope))
    ang = jnp.arange(seq, dtype=F32)[:, None] * inv_freq[None, :]
    ang = jnp.concatenate([ang, ang], axis=-1)
    cos, sin = jnp.cos(ang), jnp.sin(ang)
    del half
    cs = jnp.concatenate([cos, sin], axis=1)
    qscale = (cfg.qk_nope + cfg.qk_rope) ** -0.5 * math.log2(math.e)
    qtab = jnp.concatenate([jnp.ones((seq, cfg.qk_nope), F32), cs], axis=1) * qscale
    return cs, qtab.T


def _moe_plan(idx, rank, counts, *, n, cfg):
    ne, topk, tm = cfg.n_experts, cfg.top_k, cfg.tm_moe
    nk = n * topk
    nblk = -(-(nk + ne * (tm - 1)) // tm)
    rows = nblk * tm
    padded = (counts + tm - 1) // tm * tm
    pad_ends = jnp.cumsum(padded)
    pad_starts = pad_ends - padded
    dest = pad_starts[idx] + rank
    tok = jnp.broadcast_to(jnp.arange(n, dtype=jnp.int32)[None, :], (topk, n))
    buf_tok = jnp.zeros((rows,), jnp.int32).at[dest.reshape(-1)].set(
        tok.reshape(-1), mode="promise_in_bounds", unique_indices=True)
    blk_start = jnp.arange(nblk, dtype=jnp.int32) * tm
    blk_e = jnp.minimum(jnp.searchsorted(pad_ends, blk_start, side="right"),
                        ne - 1).astype(jnp.int32)
    n_used = (pad_ends[-1] // tm).astype(jnp.int32).reshape(1)
    return dest, buf_tok, blk_e, n_used


def _trunk(x, p, cfg):
    c = cfg
    bsz, seq, dm = x.shape
    n = bsz * seq
    xf = x.reshape(n, dm)
    hs = c.d_inner // c.ssm_head_dim
    conv_dim = c.d_inner + 2 * c.n_groups * c.d_state
    tm = min(c.tm, seq)

    wa_cols = p["w_a"].shape[1]
    proj_a = _rms_matmul(xf, 0, dm, p["norm_mix_w"], p["w_a"], F32, tm=tm, tn=wa_cols, eps=c.eps,
                         name="in_proj_a")
    wb_cols = p["w_b"].shape[1]
    tn_b = 1024 if wb_cols % 1024 == 0 else 512
    proj_b = _rms_matmul(xf, 0, dm, p["norm_mix_w"], p["w_b"], BF16, tm=tm, tn=tn_b, eps=c.eps,
                         name="in_proj_b")
    col_kr = (c.q_lora + c.kv_lora) // LANES
    col_dt = col_kr + 1

    xbc = _conv_silu(proj_b, c.d_inner, p["conv_w"], p["conv_b"], bsz=bsz, seq=seq, cfg=c)
    y_f = _ssd_scan(xbc, proj_a, col_dt, p["dt_bias2"], p["a2"], reverse=False,
                    bsz=bsz, seq=seq, cfg=c)
    y_b = _ssd_scan(xbc, proj_a, col_dt, p["dt_bias2"], p["a2"], reverse=True,
                    bsz=bsz, seq=seq, cfg=c)

    cs, qtab = _rope_tabs(c, seq)
    assert c.q_lora == c.kv_lora
    qt = _rms_matmul_t(proj_a, 0, c.q_lora, p["q_norm_w"], p["wq2t"], BF16, tm=tm,
                       tn=min(1024, p["wq2t"].shape[0]), eps=c.eps, name="q_proj_t",
                       tab_t=qtab, seq=seq)
    kn = _rms_matmul(proj_a, 1, c.kv_lora, p["kv_norm_w"], p["wk"], BF16, tm=tm,
                     tn=min(1024, p["wk"].shape[1]), eps=c.eps, name="k_proj")
    vt = _rms_matmul_t(proj_a, 1, c.kv_lora, p["kv_norm_w"], p["wvt"], BF16, tm=tm,
                       tn=min(1024, p["wvt"].shape[0]), eps=c.eps, name="v_proj_t")
    kdup = _kdup(proj_a, col_kr, cs, seq=seq, tm=tm)
    attn = _flash(qt, kn, kdup, vt, bsz=bsz, seq=seq, cfg=c)
    ga_col0 = c.d_inner + conv_dim
    oa = _gated_matmul(attn, p["w_br_attn"], proj_b, ga_col0, tm=tm, tn=min(512, dm))

    merged = _ssm_out(y_f, y_b, xbc, proj_b, ga_col0 + dm, p["dskip_x"], p["ssm_norm_w"],
                      p["w_br_ssm"], oa, cfg=c)
    h, hn = _out_proj(xf, merged, p["w_out"], p["norm_ffn_w"], cfg=c)

    idx, gates, rank, cnt = _router(hn, p["w_router_t"], p["b_router"], cfg=c)
    dest, buf_tok, blk_e, n_used = _moe_plan(idx, rank, cnt[:, 0], n=n, cfg=c)
    xb = hn.at[buf_tok].get(mode="promise_in_bounds")
    yb = _experts(xb, blk_e, n_used, p["w_gu"], p["b_gu"], p["w_down"], p["b_down"], cfg=c)
    yk = yb.at[dest.reshape(-1)].get(mode="promise_in_bounds").reshape(c.top_k, n, dm)
    out = _combine(h, yk, gates.T, p["norm_final_w"], cfg=c)
    del hs
    return out.reshape(bsz, seq, dm)


def _forward(cfg, x_prompt, x_sample, *weights):
    depth = weights[0].shape[0]
    assert depth == 1
    names_per_layer = [w[0] for w in weights[:-1]]
    p = _prep(cfg, *names_per_layer, weights[-1])
    return _trunk(x_prompt, p, cfg), _trunk(x_sample, p, cfg)


def kernel(x_prompt, x_sample, norm_mix_w, w_in, q_norm_w, kv_norm_w, w_uq, w_ukv, conv_w, conv_b,
           dt_bias_f, dt_bias_b, a_log_f, a_log_b, d_skip, ssm_norm_w, w_br_attn, w_br_ssm,
           w_out, norm_ffn_w, w_router, b_router, w_gu, b_gu, w_down, b_down, norm_final_w):
    return _forward(Cfg(), x_prompt, x_sample, norm_mix_w, w_in, q_norm_w, kv_norm_w, w_uq, w_ukv,
                    conv_w, conv_b, dt_bias_f, dt_bias_b, a_log_f, a_log_b, d_skip, ssm_norm_w,
                    w_br_attn, w_br_ssm, w_out, norm_ffn_w, w_router, b_router, w_gu, b_gu,
                    w_down, b_down, norm_final_w)
```

```python
import functools
import math
from typing import NamedTuple

import jax
import jax.numpy as jnp
from jax import lax
from jax.experimental import pallas as pl
from jax.experimental.pallas import tpu as pltpu

F32 = jnp.float32
BF16 = jnp.bfloat16
LANES = 128
HALF = 64


class Cfg(NamedTuple):
    d_model: int = 2048
    n_heads: int = 16
    q_lora: int = 512
    kv_lora: int = 512
    qk_nope: int = 128
    qk_rope: int = 64
    v_head: int = 128
    rope_theta: float = 10000.0
    d_inner: int = 4096
    ssm_head_dim: int = 64
    n_groups: int = 8
    d_state: int = 128
    conv_k: int = 5
    chunk: int = 128
    n_experts: int = 32
    top_k: int = 4
    d_ff: int = 2048
    limit: float = 7.0
    alpha: float = 1.702
    eps: float = 1e-6
    tm: int = 512
    tm_in: int = 1024
    tq: int = 512
    tk: int = 512
    flash_unroll: int = 4
    flash_heads: int = 1
    t_conv: int = 512
    tc_conv: int = 2048
    tm_gate: int = 256
    tm_out: int = 256
    tm_route: int = 512
    tm_moe: int = 512
    tf_moe: int = 512
    tm_comb: int = 256


def _cp(sem, vmem_mb=None):
    kw = dict(dimension_semantics=sem)
    if vmem_mb is not None:
        kw["vmem_limit_bytes"] = vmem_mb * 1024 * 1024
    return pltpu.CompilerParams(**kw)


def _rms_mm_kernel(x_ref, nw_ref, *rest, eps, streams):
    w_refs, (o_ref, xn_ref) = rest[:streams], rest[streams:]

    @pl.when(pl.program_id(1) == 0)
    def _():
        x = x_ref[...].astype(F32)
        ms = jnp.mean(x * x, axis=-1, keepdims=True)
        xn_ref[...] = (x * lax.rsqrt(ms + eps) * nw_ref[...]).astype(BF16)

    tw = o_ref.shape[1] // streams
    for s, w_ref in enumerate(w_refs):
        acc = jnp.dot(xn_ref[...], w_ref[...], preferred_element_type=F32)
        o_ref[:, s * tw:(s + 1) * tw] = acc.astype(o_ref.dtype)


def _rms_matmul(x, x_colblk, k, norm_w, w, out_dtype, *, tm, tn, eps, name, streams=1):
    n = x.shape[0]
    nout = w.shape[1]
    assert n % tm == 0 and nout % tn == 0 and w.shape[0] == k and tn % (streams * LANES) == 0
    tw = tn // streams
    in_specs = [
        pl.BlockSpec((tm, k), lambda i, j: (i, x_colblk)),
        pl.BlockSpec((1, k), lambda i, j: (0, 0)),
    ]
    for s in range(streams):
        in_specs.append(pl.BlockSpec((k, tw), lambda i, j, s=s: (0, j * streams + s)))
    return pl.pallas_call(
        functools.partial(_rms_mm_kernel, eps=eps, streams=streams),
        grid=(n // tm, nout // tn),
        in_specs=in_specs,
        out_specs=pl.BlockSpec((tm, tn), lambda i, j: (i, j)),
        out_shape=jax.ShapeDtypeStruct((n, nout), out_dtype),
        scratch_shapes=[pltpu.VMEM((tm, k), BF16)],
        compiler_params=_cp(("parallel", "arbitrary"), 48),
        name=name,
    )(x, norm_w.reshape(1, k).astype(F32), *([w] * streams))


def _rms_mm_t_kernel(x_ref, nw_ref, wt_ref, *rest, eps, tab_reps):
    if tab_reps:
        tab_ref, o_ref, xn_ref = rest
    else:
        o_ref, xn_ref = rest

    @pl.when(pl.program_id(1) == 0)
    def _():
        x = x_ref[...].astype(F32)
        ms = jnp.mean(x * x, axis=-1, keepdims=True)
        xn_ref[...] = (x * lax.rsqrt(ms + eps) * nw_ref[...]).astype(BF16)

    acc = lax.dot_general(wt_ref[...], xn_ref[...], (((1,), (1,)), ((), ())),
                          preferred_element_type=F32)
    if tab_reps:
        t = tab_ref[...]
        if tab_reps > 1:
            t = jnp.concatenate([t] * tab_reps, axis=0)
        acc = acc * t
    o_ref[...] = acc.astype(o_ref.dtype)


def _rms_matmul_t(x, x_colblk, k, norm_w, wt, out_dtype, *, tm, tn, eps, name, tab_t=None,
                  seq=None):
    n = x.shape[0]
    nout = wt.shape[0]
    assert n % tm == 0 and nout % tn == 0 and wt.shape[1] == k
    in_specs = [
        pl.BlockSpec((tm, k), lambda i, j: (i, x_colblk)),
        pl.BlockSpec((1, k), lambda i, j: (0, 0)),
        pl.BlockSpec((tn, k), lambda i, j: (j, 0)),
    ]
    args = [x, norm_w.reshape(1, k).astype(F32), wt]
    tab_reps = 0
    if tab_t is not None:
        tw = tab_t.shape[0]
        assert tn % tw == 0 and seq % tm == 0
        tab_reps = tn // tw
        nst = seq // tm
        in_specs.append(pl.BlockSpec((tw, tm), lambda i, j: (0, i % nst)))
        args.append(tab_t)
    return pl.pallas_call(
        functools.partial(_rms_mm_t_kernel, eps=eps, tab_reps=tab_reps),
        grid=(n // tm, nout // tn),
        in_specs=in_specs,
        out_specs=pl.BlockSpec((tn, tm), lambda i, j: (j, i)),
        out_shape=jax.ShapeDtypeStruct((nout, n), out_dtype),
        scratch_shapes=[pltpu.VMEM((tm, k), BF16)],
        compiler_params=_cp(("parallel", "arbitrary"), 48),
        name=name,
    )(*args)


HALO = 16


def _conv_kernel(prev_ref, x_ref, next_ref, w_ref, b_ref, o_ref, buf_ref, *, t, nt, kk):
    ti = pl.program_id(1)
    pv = prev_ref[...].astype(F32)[HALO - 8:, :]
    nx = next_ref[...].astype(F32)[:8, :]
    pv = jnp.where(ti == 0, 0.0, pv)
    nx = jnp.where(ti == nt - 1, 0.0, nx)
    buf_ref[0:8, :] = pv
    buf_ref[8:t + 8, :] = x_ref[...].astype(F32)
    buf_ref[t + 8:t + 16, :] = nx
    acc = b_ref[...]
    half = kk // 2
    for k in range(kk):
        acc = acc + w_ref[k:k + 1, :] * buf_ref[pl.ds(8 - half + k, t), :]
    o_ref[...] = (acc * jax.nn.sigmoid(acc)).astype(o_ref.dtype)


def _conv_silu(proj_b, col0, conv_w, conv_b, *, bsz, seq, cfg):
    n = proj_b.shape[0]
    cdim = conv_w.shape[1]
    t, tc = min(cfg.t_conv, seq), cfg.tc_conv
    assert seq % t == 0 and cdim % tc == 0 and col0 % tc == 0 and t % HALO == 0
    nt = seq // t
    cb0 = col0 // tc
    nrb = n // HALO
    rb = t // HALO

    def prev_map(b, ti, j):
        return (jnp.maximum((b * nt + ti) * rb - 1, 0), cb0 + j)

    def next_map(b, ti, j):
        return (jnp.minimum((b * nt + ti + 1) * rb, nrb - 1), cb0 + j)

    return pl.pallas_call(
        functools.partial(_conv_kernel, t=t, nt=nt, kk=cfg.conv_k),
        grid=(bsz, nt, cdim // tc),
        in_specs=[
            pl.BlockSpec((HALO, tc), prev_map),
            pl.BlockSpec((t, tc), lambda b, ti, j: (b * nt + ti, cb0 + j)),
            pl.BlockSpec((HALO, tc), next_map),
            pl.BlockSpec((cfg.conv_k, tc), lambda b, ti, j: (0, j)),
            pl.BlockSpec((1, tc), lambda b, ti, j: (0, j)),
        ],
        out_specs=pl.BlockSpec((t, tc), lambda b, ti, j: (b * nt + ti, j)),
        out_shape=jax.ShapeDtypeStruct((n, cdim), BF16),
        scratch_shapes=[pltpu.VMEM((t + 16, tc), F32)],
        compiler_params=_cp(("parallel", "parallel", "parallel")),
        name="conv_silu",
    )(proj_b, proj_b, proj_b, conv_w.astype(F32), conv_b.reshape(1, cdim).astype(F32))


def _ssd_kernel(xs_ref, b_ref, c_ref, dt_ref, bias_ref, a_ref, y_ref, st_ref, *,
                reverse, ll, ng, rr, pp, nn):
    assert pp == HALF and rr % 2 == 0 and 2 * ng * rr == LANES
    ci = pl.program_id(1)

    @pl.when(ci == 0)
    def _():
        st_ref[...] = jnp.zeros_like(st_ref)

    dcol = ng * rr if reverse else 0
    z = dt_ref[...] + bias_ref[...]
    dt_all = jnp.maximum(z, 0.0) + jnp.log(1.0 + jnp.exp(-jnp.abs(z)))
    dta = dt_all * a_ref[...]
    row = lax.broadcasted_iota(jnp.int32, (ll, ll), 0)
    col = lax.broadcasted_iota(jnp.int32, (ll, ll), 1)
    mask = (row <= col) if reverse else (row >= col)
    tri = mask.astype(F32)
    acum = jnp.dot(tri, dta, preferred_element_type=F32, precision=lax.Precision.HIGHEST)
    total = acum[0:1, :] if reverse else acum[ll - 1:ll, :]
    eac = jnp.exp(acum)
    wend = jnp.exp(total - acum) * dt_all
    etot = jnp.exp(total)
    acum_t = acum.T
    dt_t = dt_all.T
    lane = lax.broadcasted_iota(jnp.int32, (ll, LANES), 1)
    lo = lane < HALF
    lane1 = lax.broadcasted_iota(jnp.int32, (1, LANES), 1)
    lo1 = lane1 < HALF

    for g in range(ng):
        bg = b_ref[:, g * nn:(g + 1) * nn]
        cg = c_ref[:, g * nn:(g + 1) * nn]
        cb = lax.dot_general(cg, bg, (((1,), (1,)), ((), ())), preferred_element_type=F32)
        st = st_ref[g]
        gw = rr * pp
        yint = jnp.dot(cg, st.astype(BF16), preferred_element_type=F32)
        xw_parts = []
        et_parts = []
        for pr in range(rr // 2):
            c0 = dcol + g * rr + 2 * pr
            lanes0 = g * gw + pr * LANES
            xp = xs_ref[:, lanes0:lanes0 + LANES]
            ys = []
            for c in (c0, c0 + 1):
                seg = acum[:, c:c + 1] - acum_t[c:c + 1, :]
                dec = jnp.exp(jnp.where(mask, seg, -jnp.inf))
                m = (cb * dec * dt_t[c:c + 1, :]).astype(BF16)
                ys.append(jnp.dot(m, xp, preferred_element_type=F32))
            e_sel = jnp.where(lo, eac[:, c0:c0 + 1], eac[:, c0 + 1:c0 + 2])
            y = jnp.where(lo, ys[0], ys[1]) + yint[:, pr * LANES:(pr + 1) * LANES] * e_sel
            y_ref[:, lanes0:lanes0 + LANES] = y.astype(y_ref.dtype)
            w_sel = jnp.where(lo, wend[:, c0:c0 + 1], wend[:, c0 + 1:c0 + 2])
            xw_parts.append((xp.astype(F32) * w_sel).astype(BF16))
            et_parts.append(jnp.where(lo1, etot[:, c0:c0 + 1], etot[:, c0 + 1:c0 + 2]))
        xw = jnp.concatenate(xw_parts, axis=1)
        et = jnp.concatenate(et_parts, axis=1)
        upd = lax.dot_general(bg, xw, (((0,), (0,)), ((), ())), preferred_element_type=F32)
        st_ref[g] = st * et + upd


def _ssd_scan(xbc, dt_src, dt_colblk, bias2, a2, *, reverse, bsz, seq, cfg):
    n = xbc.shape[0]
    ll, ng, pp, nn = cfg.chunk, cfg.n_groups, cfg.ssm_head_dim, cfg.d_state
    di = cfg.d_inner
    rr = di // pp // ng
    nc = seq // ll
    gn = ng * nn
    assert di % gn == 0
    boff = di // gn

    def rblk(b, c):
        return b * nc + ((nc - 1 - c) if reverse else c)

    return pl.pallas_call(
        functools.partial(_ssd_kernel, reverse=reverse, ll=ll, ng=ng, rr=rr, pp=pp, nn=nn),
        grid=(bsz, nc),
        in_specs=[
            pl.BlockSpec((ll, di), lambda b, c: (rblk(b, c), 0)),
            pl.BlockSpec((ll, gn), lambda b, c: (rblk(b, c), boff)),
            pl.BlockSpec((ll, gn), lambda b, c: (rblk(b, c), boff + 1)),
            pl.BlockSpec((ll, LANES), lambda b, c: (rblk(b, c), dt_colblk)),
            pl.BlockSpec((1, LANES), lambda b, c: (0, 0)),
            pl.BlockSpec((1, LANES), lambda b, c: (0, 0)),
        ],
        out_specs=pl.BlockSpec((ll, di), lambda b, c: (rblk(b, c), 0)),
        out_shape=jax.ShapeDtypeStruct((n, di), BF16),
        scratch_shapes=[pltpu.VMEM((ng, nn, rr * pp), F32)],
        compiler_params=_cp(("parallel", "arbitrary")),
        name="ssd_bwd" if reverse else "ssd_fwd",
    )(xbc, xbc, xbc, dt_src, bias2, a2)


def _ssm_out_kernel(yf_ref, yb_ref, xs_ref, z_ref, dsk_ref, nw_ref, w_ref, gs_ref, oa_ref,
                    o_ref, yn_ref, *, eps, ng):
    @pl.when(pl.program_id(1) == 0)
    def _():
        di = yn_ref.shape[1]
        gw = di // ng
        for g in range(ng):
            sl = slice(g * gw, (g + 1) * gw)
            y = (yf_ref[:, sl].astype(F32) + yb_ref[:, sl].astype(F32)
                 + dsk_ref[:, sl] * xs_ref[:, sl].astype(F32))
            zz = z_ref[:, sl].astype(F32)
            y = y * (zz * jax.nn.sigmoid(zz))
            ms = jnp.mean(y * y, axis=-1, keepdims=True)
            yn_ref[:, sl] = (y * lax.rsqrt(ms + eps) * nw_ref[:, sl]).astype(BF16)

    acc = jnp.dot(yn_ref[...], w_ref[...], preferred_element_type=F32)
    gate = jax.nn.sigmoid(gs_ref[...].astype(F32))
    o_ref[...] = (oa_ref[...].astype(F32) + gate * acc).astype(o_ref.dtype)


def _ssm_out(yf, yb, xbc, proj_b, gs_col0, dskip_x, norm_w, w_br, oa, *, cfg):
    n = yf.shape[0]
    di, dm = cfg.d_inner, cfg.d_model
    tm, tn = cfg.tm_gate, dm
    assert n % tm == 0 and dm % tn == 0 and gs_col0 % tn == 0
    gsb = gs_col0 // tn
    return pl.pallas_call(
        functools.partial(_ssm_out_kernel, eps=cfg.eps, ng=cfg.n_groups),
        grid=(n // tm, dm // tn),
        in_specs=[
            pl.BlockSpec((tm, di), lambda i, j: (i, 0)),
            pl.BlockSpec((tm, di), lambda i, j: (i, 0)),
            pl.BlockSpec((tm, di), lambda i, j: (i, 0)),
            pl.BlockSpec((tm, di), lambda i, j: (i, 0)),
            pl.BlockSpec((1, di), lambda i, j: (0, 0)),
            pl.BlockSpec((1, di), lambda i, j: (0, 0)),
            pl.BlockSpec((di, tn), lambda i, j: (0, j), pipeline_mode=pl.Buffered(1)),
            pl.BlockSpec((tm, tn), lambda i, j: (i, gsb + j)),
            pl.BlockSpec((tm, tn), lambda i, j: (i, j)),
        ],
        out_specs=pl.BlockSpec((tm, tn), lambda i, j: (i, j)),
        out_shape=jax.ShapeDtypeStruct((n, dm), BF16),
        scratch_shapes=[pltpu.VMEM((tm, di), BF16)],
        compiler_params=_cp(("parallel", "arbitrary"), 48),
        name="ssm_out",
    )(yf, yb, xbc, proj_b, dskip_x, norm_w, w_br, proj_b, oa)


def _gated_mm_kernel(x_ref, w_ref, g_ref, o_ref):
    acc = jnp.dot(x_ref[...], w_ref[...], preferred_element_type=F32)
    o_ref[...] = (jax.nn.sigmoid(g_ref[...].astype(F32)) * acc).astype(o_ref.dtype)


def _gated_matmul(x, w, gate_src, gate_col0, *, tm, tn):
    n, k = x.shape
    nout = w.shape[1]
    assert n % tm == 0 and nout % tn == 0 and gate_col0 % tn == 0
    gb = gate_col0 // tn
    return pl.pallas_call(
        _gated_mm_kernel,
        grid=(n // tm, nout // tn),
        in_specs=[
            pl.BlockSpec((tm, k), lambda i, j: (i, 0)),
            pl.BlockSpec((k, tn), lambda i, j: (0, j)),
            pl.BlockSpec((tm, tn), lambda i, j: (i, gb + j)),
        ],
        out_specs=pl.BlockSpec((tm, tn), lambda i, j: (i, j)),
        out_shape=jax.ShapeDtypeStruct((n, nout), BF16),
        compiler_params=_cp(("parallel", "arbitrary"), 48),
        name="attn_br",
    )(x, w, gate_src)


def _kdup_kernel(kr_ref, cs_ref, o_ref):
    k2 = kr_ref[...].astype(F32) * cs_ref[...]
    o_ref[...] = (k2 + pltpu.roll(k2, HALF, 1)).astype(o_ref.dtype)


def _kdup(proj_a, colblk, cs, *, seq, tm):
    n = proj_a.shape[0]
    tm = min(tm, seq)
    nst = seq // tm
    return pl.pallas_call(
        _kdup_kernel,
        grid=(n // tm,),
        in_specs=[
            pl.BlockSpec((tm, LANES), lambda i: (i, colblk)),
            pl.BlockSpec((tm, LANES), lambda i: (i % nst, 0)),
        ],
        out_specs=pl.BlockSpec((tm, LANES), lambda i: (i, 0)),
        out_shape=jax.ShapeDtypeStruct((n, LANES), BF16),
        compiler_params=_cp(("parallel",)),
        name="kdup",
    )(proj_a, cs)


ONES_ROWS = 16


def _flash_kernel(qt_ref, kn_ref, kd_ref, vt_ref, o_ref, s_ref, m_ref, acc_ref, *,
                  tk, nk, unroll, hps):
    dq = qt_ref.shape[0] // hps
    dv = vt_ref.shape[0] // hps
    dn = kn_ref.shape[1] // hps
    ones = jnp.ones((ONES_ROWS, tk), BF16)

    def scores(hd, c):
        off = pl.multiple_of(c * tk, tk)
        kcat = jnp.concatenate([kn_ref[pl.ds(off, tk), hd * dn:(hd + 1) * dn],
                                kd_ref[pl.ds(off, tk), :]], axis=1)
        return jnp.dot(kcat, qt_ref[hd * dq:(hd + 1) * dq, :],
                       preferred_element_type=F32)

    def update(hd, c, s):
        off = pl.multiple_of(c * tk, tk)
        m = m_ref[hd]
        m_new = jnp.maximum(m, jnp.max(s, axis=0, keepdims=True))
        alpha = jnp.exp2(m - m_new)
        p = jnp.exp2((s - m_new).astype(BF16))
        m_ref[hd] = m_new
        lhs = jnp.concatenate([vt_ref[hd * dv:(hd + 1) * dv, pl.ds(off, tk)], ones], axis=0)
        acc_ref[hd] = alpha * acc_ref[hd] + jnp.dot(lhs, p, preferred_element_type=F32)

    m_ref[...] = jnp.full(m_ref.shape, -jnp.inf, F32)
    acc_ref[...] = jnp.zeros(acc_ref.shape, F32)
    for hd in range(hps):
        s_ref[hd] = scores(hd, 0)

    def trip(c, last):
        s_cur = [s_ref[hd] for hd in range(hps)]
        for u in range(unroll):
            for hd in range(hps):
                s_next = None if (last and u == unroll - 1) else scores(hd, c + u + 1)
                update(hd, c + u, s_cur[hd])
                s_cur[hd] = s_next
        if not last:
            for hd in range(hps):
                s_ref[hd] = s_cur[hd]

    def body(ci, carry):
        trip(ci * unroll, False)
        return carry

    lax.fori_loop(0, nk // unroll - 1, body, 0)
    trip(nk - unroll, True)
    for hd in range(hps):
        o_ref[:, hd * dv:(hd + 1) * dv] = (
            acc_ref[hd, 0:dv, :] / acc_ref[hd, dv:dv + 1, :]).T.astype(o_ref.dtype)


def _flash(qt, kn, kdup, vt, *, bsz, seq, cfg):
    n = qt.shape[1]
    h = cfg.n_heads
    tq, tk = min(cfg.tq, seq), min(cfg.tk, seq)
    nq = seq // tq
    dq = qt.shape[0] // h
    dn, dv = cfg.qk_nope, cfg.v_head
    assert dn == LANES and dv == LANES and dq == 2 * LANES
    nk = seq // tk
    unroll = math.gcd(nk, cfg.flash_unroll)
    hps = math.gcd(h, cfg.flash_heads)
    return pl.pallas_call(
        functools.partial(_flash_kernel, tk=tk, nk=nk, unroll=unroll, hps=hps),
        grid=(bsz, h // hps, nq),
        in_specs=[
            pl.BlockSpec((hps * dq, tq), lambda b, hh, i: (hh, b * nq + i)),
            pl.BlockSpec((seq, hps * dn), lambda b, hh, i: (b, hh)),
            pl.BlockSpec((seq, LANES), lambda b, hh, i: (b, 0)),
            pl.BlockSpec((hps * dv, seq), lambda b, hh, i: (hh, b)),
        ],
        out_specs=pl.BlockSpec((tq, hps * dv), lambda b, hh, i: (b * nq + i, hh)),
        out_shape=jax.ShapeDtypeStruct((n, h * dv), BF16),
        scratch_shapes=[pltpu.VMEM((hps, tk, tq), F32), pltpu.VMEM((hps, 1, tq), F32),
                        pltpu.VMEM((hps, dv + ONES_ROWS, tq), F32)],
        compiler_params=_cp(("parallel", "parallel", "arbitrary"), 48),
        name="flash",
    )(qt, kn, kdup, vt)


def _out_kernel(x_ref, m_ref, w_ref, nw_ref, h_ref, hn_ref, *, eps):
    hh = x_ref[...] + jnp.dot(m_ref[...], w_ref[...], preferred_element_type=F32)
    h_ref[...] = hh
    ms = jnp.mean(hh * hh, axis=-1, keepdims=True)
    hn_ref[...] = (hh * lax.rsqrt(ms + eps) * nw_ref[...]).astype(hn_ref.dtype)


def _out_proj(x, merged, w_out, norm_w, *, cfg):
    n, dm = x.shape
    tm = cfg.tm_out
    return pl.pallas_call(
        functools.partial(_out_kernel, eps=cfg.eps),
        grid=(n // tm,),
        in_specs=[
            pl.BlockSpec((tm, dm), lambda i: (i, 0)),
            pl.BlockSpec((tm, dm), lambda i: (i, 0)),
            pl.BlockSpec((dm, dm), lambda i: (0, 0)),
            pl.BlockSpec((1, dm), lambda i: (0, 0)),
        ],
        out_specs=[pl.BlockSpec((tm, dm), lambda i: (i, 0)),
                   pl.BlockSpec((tm, dm), lambda i: (i, 0))],
        out_shape=[jax.ShapeDtypeStruct((n, dm), F32), jax.ShapeDtypeStruct((n, dm), BF16)],
        compiler_params=_cp(("parallel",), 48),
        name="out_proj",
    )(x, merged, w_out, norm_w)


def _router_kernel(hn_ref, wt_ref, b_ref, idx_ref, gate_ref, rank_ref, cnt_ref, base_ref, *,
                   ne, topk):
    @pl.when(pl.program_id(0) == 0)
    def _():
        base_ref[...] = jnp.zeros_like(base_ref)

    tm = hn_ref.shape[0]
    logits = lax.dot_general(wt_ref[...], hn_ref[...], (((1,), (1,)), ((), ())),
                             preferred_element_type=F32) + b_ref[...]
    eio = lax.broadcasted_iota(jnp.int32, (ne, tm), 0)
    work = logits
    vals, onehots = [], []
    for k in range(topk):
        mx = jnp.max(work, axis=0, keepdims=True)
        sel = jnp.min(jnp.where(work == mx, eio, ne), axis=0, keepdims=True)
        oh = eio == sel
        vals.append(mx)
        onehots.append(oh)
        idx_ref[k:k + 1, :] = sel
        work = jnp.where(oh, -jnp.inf, work)
    es = [jnp.exp(v - vals[0]) for v in vals]
    den = es[0]
    for e in es[1:]:
        den = den + e
    for k in range(topk):
        gate_ref[k:k + 1, :] = es[k] / den
    a = onehots[0].astype(F32)
    for oh in onehots[1:]:
        a = a + oh.astype(F32)
    r = lax.broadcasted_iota(jnp.int32, (tm, tm), 0)
    c = lax.broadcasted_iota(jnp.int32, (tm, tm), 1)
    su = (r < c).astype(BF16)
    cum = jnp.dot(a.astype(BF16), su, preferred_element_type=F32) + base_ref[:, 0:1]
    for k in range(topk):
        rk = jnp.sum(jnp.where(onehots[k], cum, 0.0), axis=0, keepdims=True)
        rank_ref[k:k + 1, :] = rk.astype(jnp.int32)
    base_ref[...] = base_ref[...] + jnp.sum(a, axis=1, keepdims=True)
    cnt_ref[...] = base_ref[...].astype(jnp.int32)


def _router(hn, w_router_t, b_router, *, cfg):
    n, dm = hn.shape
    ne, topk = cfg.n_experts, cfg.top_k
    tm = cfg.tm_route
    assert n % tm == 0
    return pl.pallas_call(
        functools.partial(_router_kernel, ne=ne, topk=topk),
        grid=(n // tm,),
        in_specs=[
            pl.BlockSpec((tm, dm), lambda i: (i, 0)),
            pl.BlockSpec((ne, dm), lambda i: (0, 0)),
            pl.BlockSpec((ne, 1), lambda i: (0, 0)),
        ],
        out_specs=[
            pl.BlockSpec((topk, tm), lambda i: (0, i)),
            pl.BlockSpec((topk, tm), lambda i: (0, i)),
            pl.BlockSpec((topk, tm), lambda i: (0, i)),
            pl.BlockSpec((ne, LANES), lambda i: (0, 0)),
        ],
        out_shape=[
            jax.ShapeDtypeStruct((topk, n), jnp.int32),
            jax.ShapeDtypeStruct((topk, n), F32),
            jax.ShapeDtypeStruct((topk, n), jnp.int32),
            jax.ShapeDtypeStruct((ne, LANES), jnp.int32),
        ],
        scratch_shapes=[pltpu.VMEM((ne, LANES), F32)],
        compiler_params=_cp(("arbitrary",)),
        name="router",
    )(hn, w_router_t, b_router)


def _expert_kernel(be_ref, nb_ref, x_ref, wg_ref, wu_ref, bg_ref, bu_ref, wda_ref, wdb_ref, bd_ref,
                   o_ref, acc_ref, *, limit, alpha, nj):
    i = pl.program_id(0)
    j = pl.program_id(1)

    @pl.when(i < nb_ref[0])
    def _():
        x = x_ref[...]
        g = jnp.dot(x, wg_ref[...], preferred_element_type=F32) + bg_ref[...]
        u = jnp.dot(x, wu_ref[...], preferred_element_type=F32) + bu_ref[...]
        g = jnp.minimum(g, limit)
        u = jnp.clip(u, -limit, limit)
        hh = (g * jax.nn.sigmoid(alpha * g) * (u + 1.0)).astype(BF16)
        part = jnp.concatenate([jnp.dot(hh, wda_ref[...], preferred_element_type=F32),
                                jnp.dot(hh, wdb_ref[...], preferred_element_type=F32)], axis=1)

        @pl.when(j == 0)
        def _():
            acc_ref[...] = part + bd_ref[...]

        @pl.when(j > 0)
        def _():
            acc_ref[...] = acc_ref[...] + part

    @pl.when(j == nj - 1)
    def _():
        o_ref[...] = acc_ref[...].astype(o_ref.dtype)


def _experts(xb, blk_e, n_used, w_g, w_u, b_gu, wd_a, wd_b, b_down, *, cfg):
    rows, dm = xb.shape
    dff = cfg.d_ff
    tm, tf = cfg.tm_moe, min(cfg.tf_moe, dff)
    nj = dff // tf
    nblk = rows // tm
    dh = dm // 2
    grid_spec = pltpu.PrefetchScalarGridSpec(
        num_scalar_prefetch=2,
        grid=(nblk, nj),
        in_specs=[
            pl.BlockSpec((tm, dm), lambda i, j, be, nb: (i, 0)),
            pl.BlockSpec((None, dm, tf), lambda i, j, be, nb: (be[i], 0, j)),
            pl.BlockSpec((None, dm, tf), lambda i, j, be, nb: (be[i], 0, j)),
            pl.BlockSpec((None, 1, tf), lambda i, j, be, nb: (be[i], 0, j)),
            pl.BlockSpec((None, 1, tf), lambda i, j, be, nb: (be[i], 0, nj + j)),
            pl.BlockSpec((None, tf, dh), lambda i, j, be, nb: (be[i], j, 0)),
            pl.BlockSpec((None, tf, dh), lambda i, j, be, nb: (be[i], j, 0)),
            pl.BlockSpec((None, 1, dm), lambda i, j, be, nb: (be[i], 0, 0)),
        ],
        out_specs=pl.BlockSpec((tm, dm), lambda i, j, be, nb: (i, 0)),
        scratch_shapes=[pltpu.VMEM((tm, dm), F32)],
    )
    return pl.pallas_call(
        functools.partial(_expert_kernel, limit=cfg.limit, alpha=cfg.alpha, nj=nj),
        grid_spec=grid_spec,
        out_shape=jax.ShapeDtypeStruct((rows, dm), BF16),
        compiler_params=_cp(("arbitrary", "arbitrary"), 48),
        name="experts",
    )(blk_e, n_used, xb, w_g, w_u, b_gu, b_gu, wd_a, wd_b, b_down)


def _combine_kernel(h_ref, y_ref, g_ref, nw_ref, o_ref, *, eps, topk):
    hh = h_ref[...]
    g = g_ref[...]
    for k in range(topk):
        hh = hh + g[:, k:k + 1] * y_ref[k].astype(F32)
    ms = jnp.mean(hh * hh, axis=-1, keepdims=True)
    o_ref[...] = hh * lax.rsqrt(ms + eps) * nw_ref[...]


def _combine(h, yk, gates_nk, norm_w, *, cfg):
    n, dm = h.shape
    topk = cfg.top_k
    tm = cfg.tm_comb
    return pl.pallas_call(
        functools.partial(_combine_kernel, eps=cfg.eps, topk=topk),
        grid=(n // tm,),
        in_specs=[
            pl.BlockSpec((tm, dm), lambda i: (i, 0)),
            pl.BlockSpec((topk, tm, dm), lambda i: (0, i, 0)),
            pl.BlockSpec((tm, topk), lambda i: (i, 0)),
            pl.BlockSpec((1, dm), lambda i: (0, 0)),
        ],
        out_specs=pl.BlockSpec((tm, dm), lambda i: (i, 0)),
        out_shape=jax.ShapeDtypeStruct((n, dm), F32),
        compiler_params=_cp(("parallel",), 48),
        name="combine",
    )(h, yk, gates_nk, norm_w)


def _cast_kernel(xa_ref, xb_ref, oa_ref, ob_ref):
    oa_ref[...] = xa_ref[...].astype(oa_ref.dtype)
    ob_ref[...] = xb_ref[...].astype(ob_ref.dtype)


def _cast_split_bf16(w, *, block_bytes=4 * 1024 * 1024):
    e, r, c = w.shape
    ch = c // 2
    tr = max(8, min(r, block_bytes // (ch * 4)))
    assert r % tr == 0 and ch % LANES == 0
    half = jax.ShapeDtypeStruct((e, r, ch), BF16)
    return pl.pallas_call(
        _cast_kernel,
        grid=(e, r // tr),
        in_specs=[pl.BlockSpec((None, tr, ch), lambda i, j: (i, j, 0)),
                  pl.BlockSpec((None, tr, ch), lambda i, j: (i, j, 1))],
        out_specs=[pl.BlockSpec((None, tr, ch), lambda i, j: (i, j, 0)),
                   pl.BlockSpec((None, tr, ch), lambda i, j: (i, j, 0))],
        out_shape=[half, half],
        compiler_params=_cp(("parallel", "parallel"), 40),
        name="cast_bf16",
    )(w, w)


def _rot_cols(w):
    half = w.shape[-1] // 2
    return jnp.concatenate([-w[..., half:], w[..., :half]], axis=-1)


def _prep(cfg, norm_mix_w, w_in, q_norm_w, kv_norm_w, w_uq, w_ukv, conv_w, conv_b, dt_bias_f,
          dt_bias_b, a_log_f, a_log_b, d_skip, ssm_norm_w, w_br_attn, w_br_ssm, w_out,
          norm_ffn_w, w_router, b_router, w_gu, b_gu, w_down, b_down, norm_final_w):
    c = cfg
    hs = c.d_inner // c.ssm_head_dim
    conv_dim = c.d_inner + 2 * c.n_groups * c.d_state
    off_kv = c.q_lora
    off_kr = off_kv + c.kv_lora
    off_z = off_kr + c.qk_rope
    off_xbc = off_z + c.d_inner
    off_dtf = off_xbc + conv_dim
    off_dtb = off_dtf + hs
    off_ga = off_dtb + hs
    off_gs = off_ga + c.d_model
    w = w_in
    w_kr = w[:, off_kr:off_z]
    w_a = jnp.concatenate([w[:, :off_kr], w_kr, _rot_cols(w_kr), w[:, off_dtf:off_ga]], axis=1)
    w_b = jnp.concatenate([w[:, off_z:off_dtf], w[:, off_ga:]], axis=1)
    h = c.n_heads
    qk = c.qk_nope + c.qk_rope
    wq = w_uq.reshape(c.q_lora, h, qk)
    wq_r = wq[..., c.qk_nope:]
    wq2 = jnp.concatenate([wq[..., :c.qk_nope], wq_r, _rot_cols(wq_r)], axis=-1)
    wq2 = wq2.reshape(c.q_lora, h * (c.qk_nope + 2 * c.qk_rope))
    wkv = w_ukv.reshape(c.kv_lora, h, c.qk_nope + c.v_head)
    wk = wkv[..., :c.qk_nope].reshape(c.kv_lora, h * c.qk_nope)
    wvt = wkv[..., c.qk_nope:].reshape(c.kv_lora, h * c.v_head).T
    w_g, w_u = _cast_split_bf16(w_gu)
    wd_a, wd_b = _cast_split_bf16(w_down)
    return dict(
        w_g=w_g, w_u=w_u, wd_a=wd_a, wd_b=wd_b,
        norm_mix_w=norm_mix_w, w_a=w_a.astype(BF16), w_b=w_b.astype(BF16),
        q_norm_w=q_norm_w, kv_norm_w=kv_norm_w, wq2t=wq2.T.astype(BF16), wk=wk.astype(BF16),
        wvt=wvt.astype(BF16),
        conv_w=conv_w, conv_b=conv_b,
        dt_bias2=jnp.concatenate([dt_bias_f, dt_bias_b]).reshape(1, 2 * hs).astype(F32),
        a2=(-jnp.exp(jnp.concatenate([a_log_f, a_log_b]).astype(F32))).reshape(1, 2 * hs),
        dskip_x=jnp.repeat(d_skip.astype(F32), c.ssm_head_dim).reshape(1, c.d_inner),
        ssm_norm_w=ssm_norm_w.reshape(1, c.d_inner).astype(F32),
        w_br_attn=w_br_attn.astype(BF16), w_br_ssm=w_br_ssm.astype(BF16),
        w_out=w_out.astype(BF16), norm_ffn_w=norm_ffn_w.reshape(1, c.d_model).astype(F32),
        w_router_t=w_router.T.astype(BF16), b_router=b_router.reshape(c.n_experts, 1).astype(F32),
        b_gu=b_gu.reshape(c.n_experts, 1, 2 * c.d_ff).astype(F32),
        b_down=b_down.reshape(c.n_experts, 1, c.d_model).astype(F32),
        norm_final_w=norm_final_w.reshape(1, c.d_model).astype(F32),
    )


def _rope_tabs(cfg, seq):
    half = cfg.qk_rope // 2
    inv_freq = 1.0 / (cfg.rope_theta ** (jnp.arange(0, cfg.qk_rope, 2, dtype=F32) / cfg.qk_rope))
    ang = jnp.arange(seq, dtype=F32)[:, None] * inv_freq[None, :]
    ang = jnp.concatenate([ang, ang], axis=-1)
    cos, sin = jnp.cos(ang), jnp.sin(ang)
    del half
    cs = jnp.concatenate([cos, sin], axis=1)
    qscale = (cfg.qk_nope + cfg.qk_rope) ** -0.5 * math.log2(math.e)
    qtab = jnp.concatenate([jnp.ones((seq, cfg.qk_nope), F32), cs], axis=1) * qscale
    return cs, qtab.T


def _moe_plan(idx, rank, counts, *, n, cfg):
    ne, topk, tm = cfg.n_experts, cfg.top_k, cfg.tm_moe
    nk = n * topk
    nblk = -(-(nk + ne * (tm - 1)) // tm)
    rows = nblk * tm
    padded = (counts + tm - 1) // tm * tm
    pad_ends = jnp.cumsum(padded)
    pad_starts = pad_ends - padded
    dest = pad_starts[idx] + rank
    tok = jnp.broadcast_to(jnp.arange(n, dtype=jnp.int32)[None, :], (topk, n))
    buf_tok = jnp.zeros((rows,), jnp.int32).at[dest.reshape(-1)].set(
        tok.reshape(-1), mode="promise_in_bounds", unique_indices=True)
    blk_start = jnp.arange(nblk, dtype=jnp.int32) * tm
    blk_e = jnp.minimum(jnp.searchsorted(pad_ends, blk_start, side="right"),
                        ne - 1).astype(jnp.int32)
    n_used = (pad_ends[-1] // tm).astype(jnp.int32).reshape(1)
    return dest, buf_tok, blk_e, n_used


def _trunk(x, p, cfg):
    c = cfg
    bsz, seq, dm = x.shape
    n = bsz * seq
    xf = x.reshape(n, dm)
    hs = c.d_inner // c.ssm_head_dim
    conv_dim = c.d_inner + 2 * c.n_groups * c.d_state
    tm = min(c.tm, seq)

    wa_cols = p["w_a"].shape[1]
    proj_a = _rms_matmul(xf, 0, dm, p["norm_mix_w"], p["w_a"], F32, tm=tm, tn=wa_cols, eps=c.eps,
                         name="in_proj_a")
    wb_cols = p["w_b"].shape[1]
    tn_b = 1024 if wb_cols % 1024 == 0 else 512
    proj_b = _rms_matmul(xf, 0, dm, p["norm_mix_w"], p["w_b"], BF16, tm=min(c.tm_in, seq),
                         tn=tn_b, eps=c.eps, name="in_proj_b", streams=2)
    col_kr = (c.q_lora + c.kv_lora) // LANES
    col_dt = col_kr + 1

    xbc = _conv_silu(proj_b, c.d_inner, p["conv_w"], p["conv_b"], bsz=bsz, seq=seq, cfg=c)
    y_f = _ssd_scan(xbc, proj_a, col_dt, p["dt_bias2"], p["a2"], reverse=False,
                    bsz=bsz, seq=seq, cfg=c)
    y_b = _ssd_scan(xbc, proj_a, col_dt, p["dt_bias2"], p["a2"], reverse=True,
                    bsz=bsz, seq=seq, cfg=c)

    cs, qtab = _rope_tabs(c, seq)
    assert c.q_lora == c.kv_lora
    qt = _rms_matmul_t(proj_a, 0, c.q_lora, p["q_norm_w"], p["wq2t"], BF16, tm=tm,
                       tn=p["wq2t"].shape[0], eps=c.eps, name="q_proj_t",
                       tab_t=qtab, seq=seq)
    kn = _rms_matmul(proj_a, 1, c.kv_lora, p["kv_norm_w"], p["wk"], BF16, tm=tm,
                     tn=p["wk"].shape[1], eps=c.eps, name="k_proj")
    vt = _rms_matmul_t(proj_a, 1, c.kv_lora, p["kv_norm_w"], p["wvt"], BF16, tm=tm,
                       tn=p["wvt"].shape[0], eps=c.eps, name="v_proj_t")
    kdup = _kdup(proj_a, col_kr, cs, seq=seq, tm=tm)
    attn = _flash(qt, kn, kdup, vt, bsz=bsz, seq=seq, cfg=c)
    ga_col0 = c.d_inner + conv_dim
    oa = _gated_matmul(attn, p["w_br_attn"], proj_b, ga_col0, tm=tm, tn=dm)

    merged = _ssm_out(y_f, y_b, xbc, proj_b, ga_col0 + dm, p["dskip_x"], p["ssm_norm_w"],
                      p["w_br_ssm"], oa, cfg=c)
    h, hn = _out_proj(xf, merged, p["w_out"], p["norm_ffn_w"], cfg=c)

    idx, gates, rank, cnt = _router(hn, p["w_router_t"], p["b_router"], cfg=c)
    dest, buf_tok, blk_e, n_used = _moe_plan(idx, rank, cnt[:, 0], n=n, cfg=c)
    xb = hn.at[buf_tok].get(mode="promise_in_bounds")
    yb = _experts(xb, blk_e, n_used, p["w_g"], p["w_u"], p["b_gu"], p["wd_a"], p["wd_b"],
                  p["b_down"], cfg=c)
    yk = yb.at[dest.reshape(-1)].get(mode="promise_in_bounds").reshape(c.top_k, n, dm)
    out = _combine(h, yk, gates.T, p["norm_final_w"], cfg=c)
    del hs
    return out.reshape(bsz, seq, dm)


def _forward(cfg, x_prompt, x_sample, *weights):
    depth = weights[0].shape[0]
    assert depth == 1
    names_per_layer = [w[0] for w in weights[:-1]]
    p = _prep(cfg, *names_per_layer, weights[-1])
    return _trunk(x_prompt, p, cfg), _trunk(x_sample, p, cfg)


def kernel(x_prompt, x_sample, norm_mix_w, w_in, q_norm_w, kv_norm_w, w_uq, w_ukv, conv_w, conv_b,
           dt_bias_f, dt_bias_b, a_log_f, a_log_b, d_skip, ssm_norm_w, w_br_attn, w_br_ssm,
           w_out, norm_ffn_w, w_router, b_router, w_gu, b_gu, w_down, b_down, norm_final_w):
    return _forward(Cfg(), x_prompt, x_sample, norm_mix_w, w_in, q_norm_w, kv_norm_w, w_uq, w_ukv,
                    conv_w, conv_b, dt_bias_f, dt_bias_b, a_log_f, a_log_b, d_skip, ssm_norm_w,
                    w_br_attn, w_br_ssm, w_out, norm_ffn_w, w_router, b_router, w_gu, b_gu,
                    w_down, b_down, norm_final_w)
```

```python
import functools
import math
from typing import NamedTuple

import jax
import jax.numpy as jnp
from jax import lax
from jax.experimental import pallas as pl
from jax.experimental.pallas import tpu as pltpu

F32 = jnp.float32
BF16 = jnp.bfloat16
LANES = 128
HALF = 64


class Cfg(NamedTuple):
    d_model: int = 2048
    n_heads: int = 16
    q_lora: int = 512
    kv_lora: int = 512
    qk_nope: int = 128
    qk_rope: int = 64
    v_head: int = 128
    rope_theta: float = 10000.0
    d_inner: int = 4096
    ssm_head_dim: int = 64
    n_groups: int = 8
    d_state: int = 128
    conv_k: int = 5
    chunk: int = 128
    n_experts: int = 32
    top_k: int = 4
    d_ff: int = 2048
    limit: float = 7.0
    alpha: float = 1.702
    eps: float = 1e-6
    tm: int = 512
    tm_in: int = 1024
    tq: int = 512
    tk: int = 512
    flash_unroll: int = 4
    flash_heads: int = 1
    t_conv: int = 512
    tc_conv: int = 2048
    tm_gate: int = 256
    tm_out: int = 256
    tm_route: int = 512
    tm_moe: int = 512
    tf_moe: int = 512
    tm_comb: int = 256


def _cp(sem, vmem_mb=None):
    kw = dict(dimension_semantics=sem)
    if vmem_mb is not None:
        kw["vmem_limit_bytes"] = vmem_mb * 1024 * 1024
    return pltpu.CompilerParams(**kw)


def _rms_mm_kernel(x_ref, nw_ref, *rest, eps, streams):
    w_refs, (o_ref, xn_ref) = rest[:streams], rest[streams:]

    @pl.when(pl.program_id(1) == 0)
    def _():
        x = x_ref[...].astype(F32)
        ms = jnp.mean(x * x, axis=-1, keepdims=True)
        xn_ref[...] = (x * lax.rsqrt(ms + eps) * nw_ref[...]).astype(BF16)

    tw = o_ref.shape[1] // streams
    for s, w_ref in enumerate(w_refs):
        acc = jnp.dot(xn_ref[...], w_ref[...], preferred_element_type=F32)
        o_ref[:, s * tw:(s + 1) * tw] = acc.astype(o_ref.dtype)


def _rms_matmul(x, x_colblk, k, norm_w, w, out_dtype, *, tm, tn, eps, name, streams=1):
    n = x.shape[0]
    nout = w.shape[1]
    assert n % tm == 0 and nout % tn == 0 and w.shape[0] == k and tn % (streams * LANES) == 0
    tw = tn // streams
    in_specs = [
        pl.BlockSpec((tm, k), lambda i, j: (i, x_colblk)),
        pl.BlockSpec((1, k), lambda i, j: (0, 0)),
    ]
    for s in range(streams):
        in_specs.append(pl.BlockSpec((k, tw), lambda i, j, s=s: (0, j * streams + s)))
    return pl.pallas_call(
        functools.partial(_rms_mm_kernel, eps=eps, streams=streams),
        grid=(n // tm, nout // tn),
        in_specs=in_specs,
        out_specs=pl.BlockSpec((tm, tn), lambda i, j: (i, j)),
        out_shape=jax.ShapeDtypeStruct((n, nout), out_dtype),
        scratch_shapes=[pltpu.VMEM((tm, k), BF16)],
        compiler_params=_cp(("parallel", "arbitrary"), 48),
        name=name,
    )(x, norm_w.reshape(1, k).astype(F32), *([w] * streams))


def _rms_mm_t_kernel(x_ref, nw_ref, wt_ref, *rest, eps, tab_reps):
    if tab_reps:
        tab_ref, o_ref, xn_ref = rest
    else:
        o_ref, xn_ref = rest

    @pl.when(pl.program_id(1) == 0)
    def _():
        x = x_ref[...].astype(F32)
        ms = jnp.mean(x * x, axis=-1, keepdims=True)
        xn_ref[...] = (x * lax.rsqrt(ms + eps) * nw_ref[...]).astype(BF16)

    acc = lax.dot_general(wt_ref[...], xn_ref[...], (((1,), (1,)), ((), ())),
                          preferred_element_type=F32)
    if tab_reps:
        t = tab_ref[...]
        if tab_reps > 1:
            t = jnp.concatenate([t] * tab_reps, axis=0)
        acc = acc * t
    o_ref[...] = acc.astype(o_ref.dtype)


def _rms_matmul_t(x, x_colblk, k, norm_w, wt, out_dtype, *, tm, tn, eps, name, tab_t=None,
                  seq=None):
    n = x.shape[0]
    nout = wt.shape[0]
    assert n % tm == 0 and nout % tn == 0 and wt.shape[1] == k
    in_specs = [
        pl.BlockSpec((tm, k), lambda i, j: (i, x_colblk)),
        pl.BlockSpec((1, k), lambda i, j: (0, 0)),
        pl.BlockSpec((tn, k), lambda i, j: (j, 0)),
    ]
    args = [x, norm_w.reshape(1, k).astype(F32), wt]
    tab_reps = 0
    if tab_t is not None:
        tw = tab_t.shape[0]
        assert tn % tw == 0 and seq % tm == 0
        tab_reps = tn // tw
        nst = seq // tm
        in_specs.append(pl.BlockSpec((tw, tm), lambda i, j: (0, i % nst)))
        args.append(tab_t)
    return pl.pallas_call(
        functools.partial(_rms_mm_t_kernel, eps=eps, tab_reps=tab_reps),
        grid=(n // tm, nout // tn),
        in_specs=in_specs,
        out_specs=pl.BlockSpec((tn, tm), lambda i, j: (j, i)),
        out_shape=jax.ShapeDtypeStruct((nout, n), out_dtype),
        scratch_shapes=[pltpu.VMEM((tm, k), BF16)],
        compiler_params=_cp(("parallel", "arbitrary"), 48),
        name=name,
    )(*args)


HALO = 16


def _conv_kernel(prev_ref, x_ref, next_ref, w_ref, b_ref, o_ref, buf_ref, *, t, nt, kk):
    ti = pl.program_id(1)
    pv = prev_ref[...].astype(F32)[HALO - 8:, :]
    nx = next_ref[...].astype(F32)[:8, :]
    pv = jnp.where(ti == 0, 0.0, pv)
    nx = jnp.where(ti == nt - 1, 0.0, nx)
    buf_ref[0:8, :] = pv
    buf_ref[8:t + 8, :] = x_ref[...].astype(F32)
    buf_ref[t + 8:t + 16, :] = nx
    acc = b_ref[...]
    half = kk // 2
    for k in range(kk):
        acc = acc + w_ref[k:k + 1, :] * buf_ref[pl.ds(8 - half + k, t), :]
    o_ref[...] = (acc * jax.nn.sigmoid(acc)).astype(o_ref.dtype)


def _conv_silu(proj_b, col0, conv_w, conv_b, *, bsz, seq, cfg):
    n = proj_b.shape[0]
    cdim = conv_w.shape[1]
    t, tc = min(cfg.t_conv, seq), cfg.tc_conv
    assert seq % t == 0 and cdim % tc == 0 and col0 % tc == 0 and t % HALO == 0
    nt = seq // t
    cb0 = col0 // tc
    nrb = n // HALO
    rb = t // HALO

    def prev_map(b, ti, j):
        return (jnp.maximum((b * nt + ti) * rb - 1, 0), cb0 + j)

    def next_map(b, ti, j):
        return (jnp.minimum((b * nt + ti + 1) * rb, nrb - 1), cb0 + j)

    return pl.pallas_call(
        functools.partial(_conv_kernel, t=t, nt=nt, kk=cfg.conv_k),
        grid=(bsz, nt, cdim // tc),
        in_specs=[
            pl.BlockSpec((HALO, tc), prev_map),
            pl.BlockSpec((t, tc), lambda b, ti, j: (b * nt + ti, cb0 + j)),
            pl.BlockSpec((HALO, tc), next_map),
            pl.BlockSpec((cfg.conv_k, tc), lambda b, ti, j: (0, j)),
            pl.BlockSpec((1, tc), lambda b, ti, j: (0, j)),
        ],
        out_specs=pl.BlockSpec((t, tc), lambda b, ti, j: (b * nt + ti, j)),
        out_shape=jax.ShapeDtypeStruct((n, cdim), BF16),
        scratch_shapes=[pltpu.VMEM((t + 16, tc), F32)],
        compiler_params=_cp(("parallel", "parallel", "parallel")),
        name="conv_silu",
    )(proj_b, proj_b, proj_b, conv_w.astype(F32), conv_b.reshape(1, cdim).astype(F32))


def _ssd_kernel(xs_ref, b_ref, c_ref, dt_ref, bias_ref, a_ref, exp_ref, y_ref, st_ref, *,
                reverse, ll, ng, rr, pp, nn):
    assert pp == HALF and rr % 2 == 0 and 2 * ng * rr == LANES
    ci = pl.program_id(1)

    @pl.when(ci == 0)
    def _():
        st_ref[...] = jnp.zeros_like(st_ref)

    dcol = ng * rr if reverse else 0
    z = dt_ref[...] + bias_ref[...]
    dt_all = jnp.maximum(z, 0.0) + jnp.log(1.0 + jnp.exp(-jnp.abs(z)))
    dta = dt_all * a_ref[...]
    row = lax.broadcasted_iota(jnp.int32, (ll, ll), 0)
    col = lax.broadcasted_iota(jnp.int32, (ll, ll), 1)
    mask = (row <= col) if reverse else (row >= col)
    tri = mask.astype(F32)
    acum = jnp.dot(tri, dta, preferred_element_type=F32, precision=lax.Precision.HIGHEST)
    total = acum[0:1, :] if reverse else acum[ll - 1:ll, :]
    eac = jnp.exp(acum)
    wend = jnp.exp(total - acum) * dt_all
    etot = jnp.exp(total)
    acum_t = acum.T
    dt_t = dt_all.T
    lane = lax.broadcasted_iota(jnp.int32, (ll, LANES), 1)
    lo = lane < HALF
    v = jnp.concatenate([eac, wend, jnp.broadcast_to(etot, (16, LANES))], axis=0)
    v_hi = v.astype(BF16)
    v_lo = (v - v_hi.astype(F32)).astype(BF16)
    ex = jnp.dot(jnp.concatenate([v_hi, v_lo], axis=1), exp_ref[...],
                 preferred_element_type=F32)

    for g in range(ng):
        bg = b_ref[:, g * nn:(g + 1) * nn]
        cg = c_ref[:, g * nn:(g + 1) * nn]
        cb = lax.dot_general(cg, bg, (((1,), (1,)), ((), ())), preferred_element_type=F32)
        st = st_ref[g]
        gw = rr * pp
        yint = jnp.dot(cg, st.astype(BF16), preferred_element_type=F32)
        xw_parts = []
        for pr in range(rr // 2):
            c0 = dcol + g * rr + 2 * pr
            lanes0 = g * gw + pr * LANES
            xp = xs_ref[:, lanes0:lanes0 + LANES]
            ys = []
            for c in (c0, c0 + 1):
                seg = acum[:, c:c + 1] - acum_t[c:c + 1, :]
                dec = jnp.exp(jnp.where(mask, seg, -jnp.inf))
                m = (cb * dec * dt_t[c:c + 1, :]).astype(BF16)
                ys.append(jnp.dot(m, xp, preferred_element_type=F32))
            e_sel = ex[0:ll, lanes0:lanes0 + LANES]
            y = jnp.where(lo, ys[0], ys[1]) + yint[:, pr * LANES:(pr + 1) * LANES] * e_sel
            y_ref[:, lanes0:lanes0 + LANES] = y.astype(y_ref.dtype)
            w_sel = ex[ll:2 * ll, lanes0:lanes0 + LANES]
            xw_parts.append((xp.astype(F32) * w_sel).astype(BF16))
        xw = jnp.concatenate(xw_parts, axis=1)
        et = ex[2 * ll:2 * ll + 1, g * gw:(g + 1) * gw]
        upd = lax.dot_general(bg, xw, (((0,), (0,)), ((), ())), preferred_element_type=F32)
        st_ref[g] = st * et + upd


def _ssd_scan(xbc, dt_src, dt_colblk, bias2, a2, *, reverse, bsz, seq, cfg):
    n = xbc.shape[0]
    ll, ng, pp, nn = cfg.chunk, cfg.n_groups, cfg.ssm_head_dim, cfg.d_state
    di = cfg.d_inner
    rr = di // pp // ng
    nc = seq // ll
    gn = ng * nn
    assert di % gn == 0
    boff = di // gn

    def rblk(b, c):
        return b * nc + ((nc - 1 - c) if reverse else c)

    dcol = ng * rr if reverse else 0
    head_of_lane = jnp.arange(di, dtype=jnp.int32) // pp + dcol
    expand = (jnp.arange(LANES, dtype=jnp.int32)[:, None] == head_of_lane[None, :]).astype(BF16)
    expand2 = jnp.concatenate([expand, expand], axis=0)

    return pl.pallas_call(
        functools.partial(_ssd_kernel, reverse=reverse, ll=ll, ng=ng, rr=rr, pp=pp, nn=nn),
        grid=(bsz, nc),
        in_specs=[
            pl.BlockSpec((ll, di), lambda b, c: (rblk(b, c), 0)),
            pl.BlockSpec((ll, gn), lambda b, c: (rblk(b, c), boff)),
            pl.BlockSpec((ll, gn), lambda b, c: (rblk(b, c), boff + 1)),
            pl.BlockSpec((ll, LANES), lambda b, c: (rblk(b, c), dt_colblk)),
            pl.BlockSpec((1, LANES), lambda b, c: (0, 0)),
            pl.BlockSpec((1, LANES), lambda b, c: (0, 0)),
            pl.BlockSpec((2 * LANES, di), lambda b, c: (0, 0)),
        ],
        out_specs=pl.BlockSpec((ll, di), lambda b, c: (rblk(b, c), 0)),
        out_shape=jax.ShapeDtypeStruct((n, di), BF16),
        scratch_shapes=[pltpu.VMEM((ng, nn, rr * pp), F32)],
        compiler_params=_cp(("parallel", "arbitrary")),
        name="ssd_bwd" if reverse else "ssd_fwd",
    )(xbc, xbc, xbc, dt_src, bias2, a2, expand2)


def _ssm_out_kernel(yf_ref, yb_ref, xs_ref, z_ref, dsk_ref, nw_ref, w_ref, gs_ref, oa_ref,
                    o_ref, yn_ref, *, eps, ng):
    @pl.when(pl.program_id(1) == 0)
    def _():
        di = yn_ref.shape[1]
        gw = di // ng
        for g in range(ng):
            sl = slice(g * gw, (g + 1) * gw)
            y = (yf_ref[:, sl].astype(F32) + yb_ref[:, sl].astype(F32)
                 + dsk_ref[:, sl] * xs_ref[:, sl].astype(F32))
            zz = z_ref[:, sl].astype(F32)
            y = y * (zz * jax.nn.sigmoid(zz))
            ms = jnp.mean(y * y, axis=-1, keepdims=True)
            yn_ref[:, sl] = (y * lax.rsqrt(ms + eps) * nw_ref[:, sl]).astype(BF16)

    acc = jnp.dot(yn_ref[...], w_ref[...], preferred_element_type=F32)
    gate = jax.nn.sigmoid(gs_ref[...].astype(F32))
    o_ref[...] = (oa_ref[...].astype(F32) + gate * acc).astype(o_ref.dtype)


def _ssm_out(yf, yb, xbc, proj_b, gs_col0, dskip_x, norm_w, w_br, oa, *, cfg):
    n = yf.shape[0]
    di, dm = cfg.d_inner, cfg.d_model
    tm, tn = cfg.tm_gate, dm
    assert n % tm == 0 and dm % tn == 0 and gs_col0 % tn == 0
    gsb = gs_col0 // tn
    return pl.pallas_call(
        functools.partial(_ssm_out_kernel, eps=cfg.eps, ng=cfg.n_groups),
        grid=(n // tm, dm // tn),
        in_specs=[
            pl.BlockSpec((tm, di), lambda i, j: (i, 0)),
            pl.BlockSpec((tm, di), lambda i, j: (i, 0)),
            pl.BlockSpec((tm, di), lambda i, j: (i, 0)),
            pl.BlockSpec((tm, di), lambda i, j: (i, 0)),
            pl.BlockSpec((1, di), lambda i, j: (0, 0)),
            pl.BlockSpec((1, di), lambda i, j: (0, 0)),
            pl.BlockSpec((di, tn), lambda i, j: (0, j), pipeline_mode=pl.Buffered(1)),
            pl.BlockSpec((tm, tn), lambda i, j: (i, gsb + j)),
            pl.BlockSpec((tm, tn), lambda i, j: (i, j)),
        ],
        out_specs=pl.BlockSpec((tm, tn), lambda i, j: (i, j)),
        out_shape=jax.ShapeDtypeStruct((n, dm), BF16),
        scratch_shapes=[pltpu.VMEM((tm, di), BF16)],
        compiler_params=_cp(("parallel", "arbitrary"), 48),
        name="ssm_out",
    )(yf, yb, xbc, proj_b, dskip_x, norm_w, w_br, proj_b, oa)


def _gated_mm_kernel(x_ref, w_ref, g_ref, o_ref):
    acc = jnp.dot(x_ref[...], w_ref[...], preferred_element_type=F32)
    o_ref[...] = (jax.nn.sigmoid(g_ref[...].astype(F32)) * acc).astype(o_ref.dtype)


def _gated_matmul(x, w, gate_src, gate_col0, *, tm, tn):
    n, k = x.shape
    nout = w.shape[1]
    assert n % tm == 0 and nout % tn == 0 and gate_col0 % tn == 0
    gb = gate_col0 // tn
    return pl.pallas_call(
        _gated_mm_kernel,
        grid=(n // tm, nout // tn),
        in_specs=[
            pl.BlockSpec((tm, k), lambda i, j: (i, 0)),
            pl.BlockSpec((k, tn), lambda i, j: (0, j)),
            pl.BlockSpec((tm, tn), lambda i, j: (i, gb + j)),
        ],
        out_specs=pl.BlockSpec((tm, tn), lambda i, j: (i, j)),
        out_shape=jax.ShapeDtypeStruct((n, nout), BF16),
        compiler_params=_cp(("parallel", "arbitrary"), 48),
        name="attn_br",
    )(x, w, gate_src)


def _kdup_kernel(kr_ref, cs_ref, o_ref):
    k2 = kr_ref[...].astype(F32) * cs_ref[...]
    o_ref[...] = (k2 + pltpu.roll(k2, HALF, 1)).astype(o_ref.dtype)


def _kdup(proj_a, colblk, cs, *, seq, tm):
    n = proj_a.shape[0]
    tm = min(tm, seq)
    nst = seq // tm
    return pl.pallas_call(
        _kdup_kernel,
        grid=(n // tm,),
        in_specs=[
            pl.BlockSpec((tm, LANES), lambda i: (i, colblk)),
            pl.BlockSpec((tm, LANES), lambda i: (i % nst, 0)),
        ],
        out_specs=pl.BlockSpec((tm, LANES), lambda i: (i, 0)),
        out_shape=jax.ShapeDtypeStruct((n, LANES), BF16),
        compiler_params=_cp(("parallel",)),
        name="kdup",
    )(proj_a, cs)


ONES_ROWS = 16


def _flash_kernel(qt_ref, kn_ref, kd_ref, vt_ref, o_ref, s_ref, m_ref, acc_ref, *,
                  tk, nk, unroll, hps):
    dq = qt_ref.shape[0] // hps
    dv = vt_ref.shape[0] // hps
    dn = kn_ref.shape[1] // hps
    ones = jnp.ones((ONES_ROWS, tk), BF16)

    def scores(hd, c):
        off = pl.multiple_of(c * tk, tk)
        kcat = jnp.concatenate([kn_ref[pl.ds(off, tk), hd * dn:(hd + 1) * dn],
                                kd_ref[pl.ds(off, tk), :]], axis=1)
        return jnp.dot(kcat, qt_ref[hd * dq:(hd + 1) * dq, :],
                       preferred_element_type=F32)

    def update(hd, c, s):
        off = pl.multiple_of(c * tk, tk)
        m = m_ref[hd]
        m_new = jnp.maximum(m, jnp.max(s, axis=0, keepdims=True))
        alpha = jnp.exp2(m - m_new)
        p = jnp.exp2((s - m_new).astype(BF16))
        m_ref[hd] = m_new
        lhs = jnp.concatenate([vt_ref[hd * dv:(hd + 1) * dv, pl.ds(off, tk)], ones], axis=0)
        acc_ref[hd] = alpha * acc_ref[hd] + jnp.dot(lhs, p, preferred_element_type=F32)

    m_ref[...] = jnp.full(m_ref.shape, -jnp.inf, F32)
    acc_ref[...] = jnp.zeros(acc_ref.shape, F32)
    for hd in range(hps):
        s_ref[hd] = scores(hd, 0)

    def trip(c, last):
        s_cur = [s_ref[hd] for hd in range(hps)]
        for u in range(unroll):
            for hd in range(hps):
                s_next = None if (last and u == unroll - 1) else scores(hd, c + u + 1)
                update(hd, c + u, s_cur[hd])
                s_cur[hd] = s_next
        if not last:
            for hd in range(hps):
                s_ref[hd] = s_cur[hd]

    def body(ci, carry):
        trip(ci * unroll, False)
        return carry

    lax.fori_loop(0, nk // unroll - 1, body, 0)
    trip(nk - unroll, True)
    for hd in range(hps):
        o_ref[:, hd * dv:(hd + 1) * dv] = (
            acc_ref[hd, 0:dv, :] / acc_ref[hd, dv:dv + 1, :]).T.astype(o_ref.dtype)


def _flash(qt, kn, kdup, vt, *, bsz, seq, cfg):
    n = qt.shape[1]
    h = cfg.n_heads
    tq, tk = min(cfg.tq, seq), min(cfg.tk, seq)
    nq = seq // tq
    dq = qt.shape[0] // h
    dn, dv = cfg.qk_nope, cfg.v_head
    assert dn == LANES and dv == LANES and dq == 2 * LANES
    nk = seq // tk
    unroll = math.gcd(nk, cfg.flash_unroll)
    hps = math.gcd(h, cfg.flash_heads)
    return pl.pallas_call(
        functools.partial(_flash_kernel, tk=tk, nk=nk, unroll=unroll, hps=hps),
        grid=(bsz, h // hps, nq),
        in_specs=[
            pl.BlockSpec((hps * dq, tq), lambda b, hh, i: (hh, b * nq + i)),
            pl.BlockSpec((seq, hps * dn), lambda b, hh, i: (b, hh)),
            pl.BlockSpec((seq, LANES), lambda b, hh, i: (b, 0)),
            pl.BlockSpec((hps * dv, seq), lambda b, hh, i: (hh, b)),
        ],
        out_specs=pl.BlockSpec((tq, hps * dv), lambda b, hh, i: (b * nq + i, hh)),
        out_shape=jax.ShapeDtypeStruct((n, h * dv), BF16),
        scratch_shapes=[pltpu.VMEM((hps, tk, tq), F32), pltpu.VMEM((hps, 1, tq), F32),
                        pltpu.VMEM((hps, dv + ONES_ROWS, tq), F32)],
        compiler_params=_cp(("parallel", "parallel", "arbitrary"), 48),
        name="flash",
    )(qt, kn, kdup, vt)


def _out_kernel(x_ref, m_ref, w_ref, nw_ref, h_ref, hn_ref, *, eps):
    hh = x_ref[...] + jnp.dot(m_ref[...], w_ref[...], preferred_element_type=F32)
    h_ref[...] = hh
    ms = jnp.mean(hh * hh, axis=-1, keepdims=True)
    hn_ref[...] = (hh * lax.rsqrt(ms + eps) * nw_ref[...]).astype(hn_ref.dtype)


def _out_proj(x, merged, w_out, norm_w, *, cfg):
    n, dm = x.shape
    tm = cfg.tm_out
    return pl.pallas_call(
        functools.partial(_out_kernel, eps=cfg.eps),
        grid=(n // tm,),
        in_specs=[
            pl.BlockSpec((tm, dm), lambda i: (i, 0)),
            pl.BlockSpec((tm, dm), lambda i: (i, 0)),
            pl.BlockSpec((dm, dm), lambda i: (0, 0)),
            pl.BlockSpec((1, dm), lambda i: (0, 0)),
        ],
        out_specs=[pl.BlockSpec((tm, dm), lambda i: (i, 0)),
                   pl.BlockSpec((tm, dm), lambda i: (i, 0))],
        out_shape=[jax.ShapeDtypeStruct((n, dm), F32), jax.ShapeDtypeStruct((n, dm), BF16)],
        compiler_params=_cp(("parallel",), 48),
        name="out_proj",
    )(x, merged, w_out, norm_w)


def _router_kernel(hn_ref, wt_ref, b_ref, idx_ref, gate_ref, rank_ref, cnt_ref, base_ref, *,
                   ne, topk):
    @pl.when(pl.program_id(0) == 0)
    def _():
        base_ref[...] = jnp.zeros_like(base_ref)

    tm = hn_ref.shape[0]
    logits = lax.dot_general(wt_ref[...], hn_ref[...], (((1,), (1,)), ((), ())),
                             preferred_element_type=F32) + b_ref[...]
    eio = lax.broadcasted_iota(jnp.int32, (ne, tm), 0)
    work = logits
    vals, onehots = [], []
    for k in range(topk):
        mx = jnp.max(work, axis=0, keepdims=True)
        sel = jnp.min(jnp.where(work == mx, eio, ne), axis=0, keepdims=True)
        oh = eio == sel
        vals.append(mx)
        onehots.append(oh)
        idx_ref[k:k + 1, :] = sel
        work = jnp.where(oh, -jnp.inf, work)
    es = [jnp.exp(v - vals[0]) for v in vals]
    den = es[0]
    for e in es[1:]:
        den = den + e
    for k in range(topk):
        gate_ref[k:k + 1, :] = es[k] / den
    a = onehots[0].astype(F32)
    for oh in onehots[1:]:
        a = a + oh.astype(F32)
    r = lax.broadcasted_iota(jnp.int32, (tm, tm), 0)
    c = lax.broadcasted_iota(jnp.int32, (tm, tm), 1)
    su = (r < c).astype(BF16)
    cum = jnp.dot(a.astype(BF16), su, preferred_element_type=F32) + base_ref[:, 0:1]
    for k in range(topk):
        rk = jnp.sum(jnp.where(onehots[k], cum, 0.0), axis=0, keepdims=True)
        rank_ref[k:k + 1, :] = rk.astype(jnp.int32)
    base_ref[...] = base_ref[...] + jnp.sum(a, axis=1, keepdims=True)
    cnt_ref[...] = base_ref[...].astype(jnp.int32)


def _router(hn, w_router_t, b_router, *, cfg):
    n, dm = hn.shape
    ne, topk = cfg.n_experts, cfg.top_k
    tm = cfg.tm_route
    assert n % tm == 0
    return pl.pallas_call(
        functools.partial(_router_kernel, ne=ne, topk=topk),
        grid=(n // tm,),
        in_specs=[
            pl.BlockSpec((tm, dm), lambda i: (i, 0)),
            pl.BlockSpec((ne, dm), lambda i: (0, 0)),
            pl.BlockSpec((ne, 1), lambda i: (0, 0)),
        ],
        out_specs=[
            pl.BlockSpec((topk, tm), lambda i: (0, i)),
            pl.BlockSpec((topk, tm), lambda i: (0, i)),
            pl.BlockSpec((topk, tm), lambda i: (0, i)),
            pl.BlockSpec((ne, LANES), lambda i: (0, 0)),
        ],
        out_shape=[
            jax.ShapeDtypeStruct((topk, n), jnp.int32),
            jax.ShapeDtypeStruct((topk, n), F32),
            jax.ShapeDtypeStruct((topk, n), jnp.int32),
            jax.ShapeDtypeStruct((ne, LANES), jnp.int32),
        ],
        scratch_shapes=[pltpu.VMEM((ne, LANES), F32)],
        compiler_params=_cp(("arbitrary",)),
        name="router",
    )(hn, w_router_t, b_router)


def _expert_kernel(be_ref, nb_ref, x_ref, wg_ref, wu_ref, bg_ref, bu_ref, wda_ref, wdb_ref, bd_ref,
                   o_ref, acc_ref, *, limit, alpha, nj):
    i = pl.program_id(0)
    j = pl.program_id(1)

    @pl.when(i < nb_ref[0])
    def _():
        x = x_ref[...]
        g = jnp.dot(x, wg_ref[...], preferred_element_type=F32) + bg_ref[...]
        u = jnp.dot(x, wu_ref[...], preferred_element_type=F32) + bu_ref[...]
        g = jnp.minimum(g, limit)
        u = jnp.clip(u, -limit, limit)
        hh = (g * jax.nn.sigmoid(alpha * g) * (u + 1.0)).astype(BF16)
        part = jnp.concatenate(
            [jnp.dot(hh, wda_ref[...].astype(BF16), preferred_element_type=F32),
             jnp.dot(hh, wdb_ref[...].astype(BF16), preferred_element_type=F32)], axis=1)

        @pl.when(j == 0)
        def _():
            acc_ref[...] = part + bd_ref[...]

        @pl.when(j > 0)
        def _():
            acc_ref[...] = acc_ref[...] + part

    @pl.when(j == nj - 1)
    def _():
        o_ref[...] = acc_ref[...].astype(o_ref.dtype)


def _experts(xb, blk_e, n_used, w_g, w_u, b_gu, wd_a, wd_b, b_down, *, cfg):
    rows, dm = xb.shape
    dff = cfg.d_ff
    tm, tf = cfg.tm_moe, min(cfg.tf_moe, dff)
    nj = dff // tf
    nblk = rows // tm
    dh = dm // 2
    grid_spec = pltpu.PrefetchScalarGridSpec(
        num_scalar_prefetch=2,
        grid=(nblk, nj),
        in_specs=[
            pl.BlockSpec((tm, dm), lambda i, j, be, nb: (i, 0)),
            pl.BlockSpec((None, dm, tf), lambda i, j, be, nb: (be[i], 0, j)),
            pl.BlockSpec((None, dm, tf), lambda i, j, be, nb: (be[i], 0, j)),
            pl.BlockSpec((None, 1, tf), lambda i, j, be, nb: (be[i], 0, j)),
            pl.BlockSpec((None, 1, tf), lambda i, j, be, nb: (be[i], 0, nj + j)),
            pl.BlockSpec((None, tf, dh), lambda i, j, be, nb: (be[i], j, 0)),
            pl.BlockSpec((None, tf, dh), lambda i, j, be, nb: (be[i], j, 1)),
            pl.BlockSpec((None, 1, dm), lambda i, j, be, nb: (be[i], 0, 0)),
        ],
        out_specs=pl.BlockSpec((tm, dm), lambda i, j, be, nb: (i, 0)),
        scratch_shapes=[pltpu.VMEM((tm, dm), F32)],
    )
    return pl.pallas_call(
        functools.partial(_expert_kernel, limit=cfg.limit, alpha=cfg.alpha, nj=nj),
        grid_spec=grid_spec,
        out_shape=jax.ShapeDtypeStruct((rows, dm), BF16),
        compiler_params=_cp(("arbitrary", "arbitrary"), 48),
        name="experts",
    )(blk_e, n_used, xb, w_g, w_u, b_gu, b_gu, wd_a, wd_b, b_down)


def _combine_kernel(h_ref, y_ref, g_ref, nw_ref, o_ref, *, eps, topk):
    hh = h_ref[...]
    g = g_ref[...]
    for k in range(topk):
        hh = hh + g[:, k:k + 1] * y_ref[k].astype(F32)
    ms = jnp.mean(hh * hh, axis=-1, keepdims=True)
    o_ref[...] = hh * lax.rsqrt(ms + eps) * nw_ref[...]


def _combine(h, yk, gates_nk, norm_w, *, cfg):
    n, dm = h.shape
    topk = cfg.top_k
    tm = cfg.tm_comb
    return pl.pallas_call(
        functools.partial(_combine_kernel, eps=cfg.eps, topk=topk),
        grid=(n // tm,),
        in_specs=[
            pl.BlockSpec((tm, dm), lambda i: (i, 0)),
            pl.BlockSpec((topk, tm, dm), lambda i: (0, i, 0)),
            pl.BlockSpec((tm, topk), lambda i: (i, 0)),
            pl.BlockSpec((1, dm), lambda i: (0, 0)),
        ],
        out_specs=pl.BlockSpec((tm, dm), lambda i: (i, 0)),
        out_shape=jax.ShapeDtypeStruct((n, dm), F32),
        compiler_params=_cp(("parallel",), 48),
        name="combine",
    )(h, yk, gates_nk, norm_w)


def _cast_kernel(xa_ref, xb_ref, oa_ref, ob_ref):
    oa_ref[...] = xa_ref[...].astype(oa_ref.dtype)
    ob_ref[...] = xb_ref[...].astype(ob_ref.dtype)


def _cast_split_bf16(w, *, block_bytes=4 * 1024 * 1024):
    e, r, c = w.shape
    ch = c // 2
    tr = max(8, min(r, block_bytes // (ch * 4)))
    assert r % tr == 0 and ch % LANES == 0
    half = jax.ShapeDtypeStruct((e, r, ch), BF16)
    return pl.pallas_call(
        _cast_kernel,
        grid=(e, r // tr),
        in_specs=[pl.BlockSpec((None, tr, ch), lambda i, j: (i, j, 0)),
                  pl.BlockSpec((None, tr, ch), lambda i, j: (i, j, 1))],
        out_specs=[pl.BlockSpec((None, tr, ch), lambda i, j: (i, j, 0)),
                   pl.BlockSpec((None, tr, ch), lambda i, j: (i, j, 0))],
        out_shape=[half, half],
        compiler_params=_cp(("parallel", "parallel"), 40),
        name="cast_bf16",
    )(w, w)


def _rot_cols(w):
    half = w.shape[-1] // 2
    return jnp.concatenate([-w[..., half:], w[..., :half]], axis=-1)


def _prep(cfg, norm_mix_w, w_in, q_norm_w, kv_norm_w, w_uq, w_ukv, conv_w, conv_b, dt_bias_f,
          dt_bias_b, a_log_f, a_log_b, d_skip, ssm_norm_w, w_br_attn, w_br_ssm, w_out,
          norm_ffn_w, w_router, b_router, w_gu, b_gu, w_down, b_down, norm_final_w):
    c = cfg
    hs = c.d_inner // c.ssm_head_dim
    conv_dim = c.d_inner + 2 * c.n_groups * c.d_state
    off_kv = c.q_lora
    off_kr = off_kv + c.kv_lora
    off_z = off_kr + c.qk_rope
    off_xbc = off_z + c.d_inner
    off_dtf = off_xbc + conv_dim
    off_dtb = off_dtf + hs
    off_ga = off_dtb + hs
    off_gs = off_ga + c.d_model
    w = w_in
    w_kr = w[:, off_kr:off_z]
    w_a = jnp.concatenate([w[:, :off_kr], w_kr, _rot_cols(w_kr), w[:, off_dtf:off_ga]], axis=1)
    w_b = jnp.concatenate([w[:, off_z:off_dtf], w[:, off_ga:]], axis=1)
    h = c.n_heads
    qk = c.qk_nope + c.qk_rope
    wq = w_uq.reshape(c.q_lora, h, qk)
    wq_r = wq[..., c.qk_nope:]
    wq2 = jnp.concatenate([wq[..., :c.qk_nope], wq_r, _rot_cols(wq_r)], axis=-1)
    wq2 = wq2.reshape(c.q_lora, h * (c.qk_nope + 2 * c.qk_rope))
    wkv = w_ukv.reshape(c.kv_lora, h, c.qk_nope + c.v_head)
    wk = wkv[..., :c.qk_nope].reshape(c.kv_lora, h * c.qk_nope)
    wvt = wkv[..., c.qk_nope:].reshape(c.kv_lora, h * c.v_head).T
    w_g, w_u = _cast_split_bf16(w_gu)
    return dict(
        w_g=w_g, w_u=w_u, w_down=w_down,
        norm_mix_w=norm_mix_w, w_a=w_a.astype(BF16), w_b=w_b.astype(BF16),
        q_norm_w=q_norm_w, kv_norm_w=kv_norm_w, wq2t=wq2.T.astype(BF16), wk=wk.astype(BF16),
        wvt=wvt.astype(BF16),
        conv_w=conv_w, conv_b=conv_b,
        dt_bias2=jnp.concatenate([dt_bias_f, dt_bias_b]).reshape(1, 2 * hs).astype(F32),
        a2=(-jnp.exp(jnp.concatenate([a_log_f, a_log_b]).astype(F32))).reshape(1, 2 * hs),
        dskip_x=jnp.repeat(d_skip.astype(F32), c.ssm_head_dim).reshape(1, c.d_inner),
        ssm_norm_w=ssm_norm_w.reshape(1, c.d_inner).astype(F32),
        w_br_attn=w_br_attn.astype(BF16), w_br_ssm=w_br_ssm.astype(BF16),
        w_out=w_out.astype(BF16), norm_ffn_w=norm_ffn_w.reshape(1, c.d_model).astype(F32),
        w_router_t=w_router.T.astype(BF16), b_router=b_router.reshape(c.n_experts, 1).astype(F32),
        b_gu=b_gu.reshape(c.n_experts, 1, 2 * c.d_ff).astype(F32),
        b_down=b_down.reshape(c.n_experts, 1, c.d_model).astype(F32),
        norm_final_w=norm_final_w.reshape(1, c.d_model).astype(F32),
    )


def _rope_tabs(cfg, seq):
    half = cfg.qk_rope // 2
    inv_freq = 1.0 / (cfg.rope_theta ** (jnp.arange(0, cfg.qk_rope, 2, dtype=F32) / cfg.qk_rope))
    ang = jnp.arange(seq, dtype=F32)[:, None] * inv_freq[None, :]
    ang = jnp.concatenate([ang, ang], axis=-1)
    cos, sin = jnp.cos(ang), jnp.sin(ang)
    del half
    cs = jnp.concatenate([cos, sin], axis=1)
    qscale = (cfg.qk_nope + cfg.qk_rope) ** -0.5 * math.log2(math.e)
    qtab = jnp.concatenate([jnp.ones((seq, cfg.qk_nope), F32), cs], axis=1) * qscale
    return cs, qtab.T


def _moe_plan(idx, rank, counts, *, n, cfg):
    ne, topk, tm = cfg.n_experts, cfg.top_k, cfg.tm_moe
    nk = n * topk
    nblk = -(-(nk + ne * (tm - 1)) // tm)
    rows = nblk * tm
    padded = (counts + tm - 1) // tm * tm
    pad_ends = jnp.cumsum(padded)
    pad_starts = pad_ends - padded
    dest = pad_starts[idx] + rank
    tok = jnp.broadcast_to(jnp.arange(n, dtype=jnp.int32)[None, :], (topk, n))
    buf_tok = jnp.zeros((rows,), jnp.int32).at[dest.reshape(-1)].set(
        tok.reshape(-1), mode="promise_in_bounds", unique_indices=True)
    blk_start = jnp.arange(nblk, dtype=jnp.int32) * tm
    blk_e = jnp.minimum(jnp.sum((pad_ends[None, :] <= blk_start[:, None]).astype(jnp.int32), axis=1),
                        ne - 1)
    n_used = (pad_ends[-1] // tm).astype(jnp.int32).reshape(1)
    return dest, buf_tok, blk_e, n_used


def _trunk(x, p, cfg):
    c = cfg
    bsz, seq, dm = x.shape
    n = bsz * seq
    xf = x.reshape(n, dm)
    hs = c.d_inner // c.ssm_head_dim
    conv_dim = c.d_inner + 2 * c.n_groups * c.d_state
    tm = min(c.tm, seq)

    wa_cols = p["w_a"].shape[1]
    proj_a = _rms_matmul(xf, 0, dm, p["norm_mix_w"], p["w_a"], F32, tm=tm, tn=wa_cols, eps=c.eps,
                         name="in_proj_a")
    wb_cols = p["w_b"].shape[1]
    tn_b = 1024 if wb_cols % 1024 == 0 else 512
    proj_b = _rms_matmul(xf, 0, dm, p["norm_mix_w"], p["w_b"], BF16, tm=min(c.tm_in, seq),
                         tn=tn_b, eps=c.eps, name="in_proj_b", streams=2)
    col_kr = (c.q_lora + c.kv_lora) // LANES
    col_dt = col_kr + 1

    xbc = _conv_silu(proj_b, c.d_inner, p["conv_w"], p["conv_b"], bsz=bsz, seq=seq, cfg=c)
    y_f = _ssd_scan(xbc, proj_a, col_dt, p["dt_bias2"], p["a2"], reverse=False,
                    bsz=bsz, seq=seq, cfg=c)
    y_b = _ssd_scan(xbc, proj_a, col_dt, p["dt_bias2"], p["a2"], reverse=True,
                    bsz=bsz, seq=seq, cfg=c)

    cs, qtab = _rope_tabs(c, seq)
    assert c.q_lora == c.kv_lora
    qt = _rms_matmul_t(proj_a, 0, c.q_lora, p["q_norm_w"], p["wq2t"], BF16, tm=tm,
                       tn=p["wq2t"].shape[0], eps=c.eps, name="q_proj_t",
                       tab_t=qtab, seq=seq)
    kn = _rms_matmul(proj_a, 1, c.kv_lora, p["kv_norm_w"], p["wk"], BF16, tm=tm,
                     tn=p["wk"].shape[1], eps=c.eps, name="k_proj")
    vt = _rms_matmul_t(proj_a, 1, c.kv_lora, p["kv_norm_w"], p["wvt"], BF16, tm=tm,
                       tn=p["wvt"].shape[0], eps=c.eps, name="v_proj_t")
    kdup = _kdup(proj_a, col_kr, cs, seq=seq, tm=tm)
    attn = _flash(qt, kn, kdup, vt, bsz=bsz, seq=seq, cfg=c)
    ga_col0 = c.d_inner + conv_dim
    oa = _gated_matmul(attn, p["w_br_attn"], proj_b, ga_col0, tm=tm, tn=dm)

    merged = _ssm_out(y_f, y_b, xbc, proj_b, ga_col0 + dm, p["dskip_x"], p["ssm_norm_w"],
                      p["w_br_ssm"], oa, cfg=c)
    h, hn = _out_proj(xf, merged, p["w_out"], p["norm_ffn_w"], cfg=c)

    idx, gates, rank, cnt = _router(hn, p["w_router_t"], p["b_router"], cfg=c)
    dest, buf_tok, blk_e, n_used = _moe_plan(idx, rank, cnt[:, 0], n=n, cfg=c)
    xb = hn.at[buf_tok].get(mode="promise_in_bounds")
    yb = _experts(xb, blk_e, n_used, p["w_g"], p["w_u"], p["b_gu"], p["w_down"], p["w_down"],
                  p["b_down"], cfg=c)
    yk = yb.at[dest.reshape(-1)].get(mode="promise_in_bounds").reshape(c.top_k, n, dm)
    out = _combine(h, yk, gates.T, p["norm_final_w"], cfg=c)
    del hs
    return out.reshape(bsz, seq, dm)


def _forward(cfg, x_prompt, x_sample, *weights):
    depth = weights[0].shape[0]
    assert depth == 1
    names_per_layer = [w[0] for w in weights[:-1]]
    p = _prep(cfg, *names_per_layer, weights[-1])
    return _trunk(x_prompt, p, cfg), _trunk(x_sample, p, cfg)


def kernel(x_prompt, x_sample, norm_mix_w, w_in, q_norm_w, kv_norm_w, w_uq, w_ukv, conv_w, conv_b,
           dt_bias_f, dt_bias_b, a_log_f, a_log_b, d_skip, ssm_norm_w, w_br_attn, w_br_ssm,
           w_out, norm_ffn_w, w_router, b_router, w_gu, b_gu, w_down, b_down, norm_final_w):
    return _forward(Cfg(), x_prompt, x_sample, norm_mix_w, w_in, q_norm_w, kv_norm_w, w_uq, w_ukv,
                    conv_w, conv_b, dt_bias_f, dt_bias_b, a_log_f, a_log_b, d_skip, ssm_norm_w,
                    w_br_attn, w_br_ssm, w_out, norm_ffn_w, w_router, b_router, w_gu, b_gu,
                    w_down, b_down, norm_final_w)
```

```python
import functools
import math
from typing import NamedTuple

import jax
import jax.numpy as jnp
from jax import lax
from jax.experimental import pallas as pl
from jax.experimental.pallas import tpu as pltpu

F32 = jnp.float32
BF16 = jnp.bfloat16
LANES = 128
HALF = 64


class Cfg(NamedTuple):
    d_model: int = 2048
    n_heads: int = 16
    q_lora: int = 512
    kv_lora: int = 512
    qk_nope: int = 128
    qk_rope: int = 64
    v_head: int = 128
    rope_theta: float = 10000.0
    d_inner: int = 4096
    ssm_head_dim: int = 64
    n_groups: int = 8
    d_state: int = 128
    conv_k: int = 5
    chunk: int = 128
    n_experts: int = 32
    top_k: int = 4
    d_ff: int = 2048
    limit: float = 7.0
    alpha: float = 1.702
    eps: float = 1e-6
    tm: int = 512
    tm_in: int = 1024
    tq: int = 512
    tk: int = 512
    flash_unroll: int = 4
    flash_heads: int = 1
    t_conv: int = 512
    tc_conv: int = 2048
    tm_gate: int = 256
    tm_out: int = 256
    tm_route: int = 512
    tm_moe: int = 512
    tf_moe: int = 1024
    tm_comb: int = 256


def _cp(sem, vmem_mb=None):
    kw = dict(dimension_semantics=sem)
    if vmem_mb is not None:
        kw["vmem_limit_bytes"] = vmem_mb * 1024 * 1024
    return pltpu.CompilerParams(**kw)


def _rms_mm_kernel(x_ref, nw_ref, *rest, eps, streams):
    w_refs, (o_ref, xn_ref) = rest[:streams], rest[streams:]

    @pl.when(pl.program_id(1) == 0)
    def _():
        x = x_ref[...].astype(F32)
        ms = jnp.mean(x * x, axis=-1, keepdims=True)
        xn_ref[...] = (x * lax.rsqrt(ms + eps) * nw_ref[...]).astype(BF16)

    tw = o_ref.shape[1] // streams
    for s, w_ref in enumerate(w_refs):
        acc = jnp.dot(xn_ref[...], w_ref[...], preferred_element_type=F32)
        o_ref[:, s * tw:(s + 1) * tw] = acc.astype(o_ref.dtype)


def _rms_matmul(x, x_colblk, k, norm_w, w, out_dtype, *, tm, tn, eps, name, streams=1):
    n = x.shape[0]
    nout = w.shape[1]
    assert n % tm == 0 and nout % tn == 0 and w.shape[0] == k and tn % (streams * LANES) == 0
    tw = tn // streams
    in_specs = [
        pl.BlockSpec((tm, k), lambda i, j: (i, x_colblk)),
        pl.BlockSpec((1, k), lambda i, j: (0, 0)),
    ]
    for s in range(streams):
        in_specs.append(pl.BlockSpec((k, tw), lambda i, j, s=s: (0, j * streams + s)))
    return pl.pallas_call(
        functools.partial(_rms_mm_kernel, eps=eps, streams=streams),
        grid=(n // tm, nout // tn),
        in_specs=in_specs,
        out_specs=pl.BlockSpec((tm, tn), lambda i, j: (i, j)),
        out_shape=jax.ShapeDtypeStruct((n, nout), out_dtype),
        scratch_shapes=[pltpu.VMEM((tm, k), BF16)],
        compiler_params=_cp(("parallel", "arbitrary"), 48),
        name=name,
    )(x, norm_w.reshape(1, k).astype(F32), *([w] * streams))


def _rms_mm_t_kernel(x_ref, nw_ref, wt_ref, *rest, eps, tab_reps):
    if tab_reps:
        tab_ref, o_ref, xn_ref = rest
    else:
        o_ref, xn_ref = rest

    @pl.when(pl.program_id(1) == 0)
    def _():
        x = x_ref[...].astype(F32)
        ms = jnp.mean(x * x, axis=-1, keepdims=True)
        xn_ref[...] = (x * lax.rsqrt(ms + eps) * nw_ref[...]).astype(BF16)

    acc = lax.dot_general(wt_ref[...], xn_ref[...], (((1,), (1,)), ((), ())),
                          preferred_element_type=F32)
    if tab_reps:
        t = tab_ref[...]
        if tab_reps > 1:
            t = jnp.concatenate([t] * tab_reps, axis=0)
        acc = acc * t
    o_ref[...] = acc.astype(o_ref.dtype)


def _rms_matmul_t(x, x_colblk, k, norm_w, wt, out_dtype, *, tm, tn, eps, name, tab_t=None,
                  seq=None):
    n = x.shape[0]
    nout = wt.shape[0]
    assert n % tm == 0 and nout % tn == 0 and wt.shape[1] == k
    in_specs = [
        pl.BlockSpec((tm, k), lambda i, j: (i, x_colblk)),
        pl.BlockSpec((1, k), lambda i, j: (0, 0)),
        pl.BlockSpec((tn, k), lambda i, j: (j, 0)),
    ]
    args = [x, norm_w.reshape(1, k).astype(F32), wt]
    tab_reps = 0
    if tab_t is not None:
        tw = tab_t.shape[0]
        assert tn % tw == 0 and seq % tm == 0
        tab_reps = tn // tw
        nst = seq // tm
        in_specs.append(pl.BlockSpec((tw, tm), lambda i, j: (0, i % nst)))
        args.append(tab_t)
    return pl.pallas_call(
        functools.partial(_rms_mm_t_kernel, eps=eps, tab_reps=tab_reps),
        grid=(n // tm, nout // tn),
        in_specs=in_specs,
        out_specs=pl.BlockSpec((tn, tm), lambda i, j: (j, i)),
        out_shape=jax.ShapeDtypeStruct((nout, n), out_dtype),
        scratch_shapes=[pltpu.VMEM((tm, k), BF16)],
        compiler_params=_cp(("parallel", "arbitrary"), 48),
        name=name,
    )(*args)


HALO = 16


def _conv_kernel(prev_ref, x_ref, next_ref, w_ref, b_ref, o_ref, buf_ref, *, t, nt, kk):
    ti = pl.program_id(1)
    pv = prev_ref[...].astype(F32)[HALO - 8:, :]
    nx = next_ref[...].astype(F32)[:8, :]
    pv = jnp.where(ti == 0, 0.0, pv)
    nx = jnp.where(ti == nt - 1, 0.0, nx)
    buf_ref[0:8, :] = pv
    buf_ref[8:t + 8, :] = x_ref[...].astype(F32)
    buf_ref[t + 8:t + 16, :] = nx
    acc = b_ref[...]
    half = kk // 2
    for k in range(kk):
        acc = acc + w_ref[k:k + 1, :] * buf_ref[pl.ds(8 - half + k, t), :]
    o_ref[...] = (acc * jax.nn.sigmoid(acc)).astype(o_ref.dtype)


def _conv_silu(proj_b, col0, conv_w, conv_b, *, bsz, seq, cfg):
    n = proj_b.shape[0]
    cdim = conv_w.shape[1]
    t, tc = min(cfg.t_conv, seq), cfg.tc_conv
    assert seq % t == 0 and cdim % tc == 0 and col0 % tc == 0 and t % HALO == 0
    nt = seq // t
    cb0 = col0 // tc
    nrb = n // HALO
    rb = t // HALO

    def prev_map(b, ti, j):
        return (jnp.maximum((b * nt + ti) * rb - 1, 0), cb0 + j)

    def next_map(b, ti, j):
        return (jnp.minimum((b * nt + ti + 1) * rb, nrb - 1), cb0 + j)

    return pl.pallas_call(
        functools.partial(_conv_kernel, t=t, nt=nt, kk=cfg.conv_k),
        grid=(bsz, nt, cdim // tc),
        in_specs=[
            pl.BlockSpec((HALO, tc), prev_map),
            pl.BlockSpec((t, tc), lambda b, ti, j: (b * nt + ti, cb0 + j)),
            pl.BlockSpec((HALO, tc), next_map),
            pl.BlockSpec((cfg.conv_k, tc), lambda b, ti, j: (0, j)),
            pl.BlockSpec((1, tc), lambda b, ti, j: (0, j)),
        ],
        out_specs=pl.BlockSpec((t, tc), lambda b, ti, j: (b * nt + ti, j)),
        out_shape=jax.ShapeDtypeStruct((n, cdim), BF16),
        scratch_shapes=[pltpu.VMEM((t + 16, tc), F32)],
        compiler_params=_cp(("parallel", "parallel", "parallel")),
        name="conv_silu",
    )(proj_b, proj_b, proj_b, conv_w.astype(F32), conv_b.reshape(1, cdim).astype(F32))


def _ssd_kernel(xs_ref, b_ref, c_ref, dt_ref, bias_ref, a_ref, exp_ref, y_ref, st_ref, *,
                reverse, ll, ng, rr, pp, nn):
    assert pp == HALF and rr % 2 == 0 and 2 * ng * rr == LANES
    ci = pl.program_id(1)

    @pl.when(ci == 0)
    def _():
        st_ref[...] = jnp.zeros_like(st_ref)

    dcol = ng * rr if reverse else 0
    z = dt_ref[...] + bias_ref[...]
    dt_all = jnp.maximum(z, 0.0) + jnp.log(1.0 + jnp.exp(-jnp.abs(z)))
    dta = dt_all * a_ref[...]
    row = lax.broadcasted_iota(jnp.int32, (ll, ll), 0)
    col = lax.broadcasted_iota(jnp.int32, (ll, ll), 1)
    mask = (row <= col) if reverse else (row >= col)
    tri = mask.astype(F32)
    acum = jnp.dot(tri, dta, preferred_element_type=F32, precision=lax.Precision.HIGHEST)
    total = acum[0:1, :] if reverse else acum[ll - 1:ll, :]
    eac = jnp.exp(acum)
    wend = jnp.exp(total - acum) * dt_all
    etot = jnp.exp(total)
    acum_t = acum.T
    dt_t = dt_all.T
    lane = lax.broadcasted_iota(jnp.int32, (ll, LANES), 1)
    lo = lane < HALF
    v = jnp.concatenate([eac, wend, jnp.broadcast_to(etot, (16, LANES))], axis=0)
    v_hi = v.astype(BF16)
    v_lo = (v - v_hi.astype(F32)).astype(BF16)
    ex = jnp.dot(jnp.concatenate([v_hi, v_lo], axis=1), exp_ref[...],
                 preferred_element_type=F32)

    for g in range(ng):
        bg = b_ref[:, g * nn:(g + 1) * nn]
        cg = c_ref[:, g * nn:(g + 1) * nn]
        cb = lax.dot_general(cg, bg, (((1,), (1,)), ((), ())), preferred_element_type=F32)
        st = st_ref[g]
        gw = rr * pp
        yint = jnp.dot(cg, st.astype(BF16), preferred_element_type=F32)
        xw_parts = []
        for pr in range(rr // 2):
            c0 = dcol + g * rr + 2 * pr
            lanes0 = g * gw + pr * LANES
            xp = xs_ref[:, lanes0:lanes0 + LANES]
            ys = []
            for c in (c0, c0 + 1):
                seg = acum[:, c:c + 1] - acum_t[c:c + 1, :]
                dec = jnp.exp(jnp.where(mask, seg, -jnp.inf))
                m = (cb * dec * dt_t[c:c + 1, :]).astype(BF16)
                ys.append(jnp.dot(m, xp, preferred_element_type=F32))
            e_sel = ex[0:ll, lanes0:lanes0 + LANES]
            y = jnp.where(lo, ys[0], ys[1]) + yint[:, pr * LANES:(pr + 1) * LANES] * e_sel
            y_ref[:, lanes0:lanes0 + LANES] = y.astype(y_ref.dtype)
            w_sel = ex[ll:2 * ll, lanes0:lanes0 + LANES]
            xw_parts.append((xp.astype(F32) * w_sel).astype(BF16))
        xw = jnp.concatenate(xw_parts, axis=1)
        et = ex[2 * ll:2 * ll + 1, g * gw:(g + 1) * gw]
        upd = lax.dot_general(bg, xw, (((0,), (0,)), ((), ())), preferred_element_type=F32)
        st_ref[g] = st * et + upd


def _ssd_scan(xbc, dt_src, dt_colblk, bias2, a2, *, reverse, bsz, seq, cfg):
    n = xbc.shape[0]
    ll, ng, pp, nn = cfg.chunk, cfg.n_groups, cfg.ssm_head_dim, cfg.d_state
    di = cfg.d_inner
    rr = di // pp // ng
    nc = seq // ll
    gn = ng * nn
    assert di % gn == 0
    boff = di // gn

    def rblk(b, c):
        return b * nc + ((nc - 1 - c) if reverse else c)

    dcol = ng * rr if reverse else 0
    head_of_lane = jnp.arange(di, dtype=jnp.int32) // pp + dcol
    expand = (jnp.arange(LANES, dtype=jnp.int32)[:, None] == head_of_lane[None, :]).astype(BF16)
    expand2 = jnp.concatenate([expand, expand], axis=0)

    return pl.pallas_call(
        functools.partial(_ssd_kernel, reverse=reverse, ll=ll, ng=ng, rr=rr, pp=pp, nn=nn),
        grid=(bsz, nc),
        in_specs=[
            pl.BlockSpec((ll, di), lambda b, c: (rblk(b, c), 0)),
            pl.BlockSpec((ll, gn), lambda b, c: (rblk(b, c), boff)),
            pl.BlockSpec((ll, gn), lambda b, c: (rblk(b, c), boff + 1)),
            pl.BlockSpec((ll, LANES), lambda b, c: (rblk(b, c), dt_colblk)),
            pl.BlockSpec((1, LANES), lambda b, c: (0, 0)),
            pl.BlockSpec((1, LANES), lambda b, c: (0, 0)),
            pl.BlockSpec((2 * LANES, di), lambda b, c: (0, 0)),
        ],
        out_specs=pl.BlockSpec((ll, di), lambda b, c: (rblk(b, c), 0)),
        out_shape=jax.ShapeDtypeStruct((n, di), BF16),
        scratch_shapes=[pltpu.VMEM((ng, nn, rr * pp), F32)],
        compiler_params=_cp(("parallel", "arbitrary")),
        name="ssd_bwd" if reverse else "ssd_fwd",
    )(xbc, xbc, xbc, dt_src, bias2, a2, expand2)


def _ssm_out_kernel(yf_ref, yb_ref, xs_ref, z_ref, dsk_ref, nw_ref, w_ref, gs_ref, oa_ref,
                    o_ref, yn_ref, *, eps, ng):
    @pl.when(pl.program_id(1) == 0)
    def _():
        di = yn_ref.shape[1]
        gw = di // ng
        for g in range(ng):
            sl = slice(g * gw, (g + 1) * gw)
            y = (yf_ref[:, sl].astype(F32) + yb_ref[:, sl].astype(F32)
                 + dsk_ref[:, sl] * xs_ref[:, sl].astype(F32))
            zz = z_ref[:, sl].astype(F32)
            y = y * (zz * jax.nn.sigmoid(zz))
            ms = jnp.mean(y * y, axis=-1, keepdims=True)
            yn_ref[:, sl] = (y * lax.rsqrt(ms + eps) * nw_ref[:, sl]).astype(BF16)

    acc = jnp.dot(yn_ref[...], w_ref[...], preferred_element_type=F32)
    gate = jax.nn.sigmoid(gs_ref[...].astype(F32))
    o_ref[...] = (oa_ref[...].astype(F32) + gate * acc).astype(o_ref.dtype)


def _ssm_out(yf, yb, xbc, proj_b, gs_col0, dskip_x, norm_w, w_br, oa, *, cfg):
    n = yf.shape[0]
    di, dm = cfg.d_inner, cfg.d_model
    tm, tn = cfg.tm_gate, dm
    assert n % tm == 0 and dm % tn == 0 and gs_col0 % tn == 0
    gsb = gs_col0 // tn
    return pl.pallas_call(
        functools.partial(_ssm_out_kernel, eps=cfg.eps, ng=cfg.n_groups),
        grid=(n // tm, dm // tn),
        in_specs=[
            pl.BlockSpec((tm, di), lambda i, j: (i, 0)),
            pl.BlockSpec((tm, di), lambda i, j: (i, 0)),
            pl.BlockSpec((tm, di), lambda i, j: (i, 0)),
            pl.BlockSpec((tm, di), lambda i, j: (i, 0)),
            pl.BlockSpec((1, di), lambda i, j: (0, 0)),
            pl.BlockSpec((1, di), lambda i, j: (0, 0)),
            pl.BlockSpec((di, tn), lambda i, j: (0, j), pipeline_mode=pl.Buffered(1)),
            pl.BlockSpec((tm, tn), lambda i, j: (i, gsb + j)),
            pl.BlockSpec((tm, tn), lambda i, j: (i, j)),
        ],
        out_specs=pl.BlockSpec((tm, tn), lambda i, j: (i, j)),
        out_shape=jax.ShapeDtypeStruct((n, dm), BF16),
        scratch_shapes=[pltpu.VMEM((tm, di), BF16)],
        compiler_params=_cp(("parallel", "arbitrary"), 48),
        name="ssm_out",
    )(yf, yb, xbc, proj_b, dskip_x, norm_w, w_br, proj_b, oa)


def _gated_mm_kernel(x_ref, w_ref, g_ref, o_ref):
    acc = jnp.dot(x_ref[...], w_ref[...], preferred_element_type=F32)
    o_ref[...] = (jax.nn.sigmoid(g_ref[...].astype(F32)) * acc).astype(o_ref.dtype)


def _gated_matmul(x, w, gate_src, gate_col0, *, tm, tn):
    n, k = x.shape
    nout = w.shape[1]
    assert n % tm == 0 and nout % tn == 0 and gate_col0 % tn == 0
    gb = gate_col0 // tn
    return pl.pallas_call(
        _gated_mm_kernel,
        grid=(n // tm, nout // tn),
        in_specs=[
            pl.BlockSpec((tm, k), lambda i, j: (i, 0)),
            pl.BlockSpec((k, tn), lambda i, j: (0, j)),
            pl.BlockSpec((tm, tn), lambda i, j: (i, gb + j)),
        ],
        out_specs=pl.BlockSpec((tm, tn), lambda i, j: (i, j)),
        out_shape=jax.ShapeDtypeStruct((n, nout), BF16),
        compiler_params=_cp(("parallel", "arbitrary"), 48),
        name="attn_br",
    )(x, w, gate_src)


def _kdup_kernel(kr_ref, cs_ref, o_ref):
    k2 = kr_ref[...].astype(F32) * cs_ref[...]
    o_ref[...] = (k2 + pltpu.roll(k2, HALF, 1)).astype(o_ref.dtype)


def _kdup(proj_a, colblk, cs, *, seq, tm):
    n = proj_a.shape[0]
    tm = min(tm, seq)
    nst = seq // tm
    return pl.pallas_call(
        _kdup_kernel,
        grid=(n // tm,),
        in_specs=[
            pl.BlockSpec((tm, LANES), lambda i: (i, colblk)),
            pl.BlockSpec((tm, LANES), lambda i: (i % nst, 0)),
        ],
        out_specs=pl.BlockSpec((tm, LANES), lambda i: (i, 0)),
        out_shape=jax.ShapeDtypeStruct((n, LANES), BF16),
        compiler_params=_cp(("parallel",)),
        name="kdup",
    )(proj_a, cs)


ONES_ROWS = 16


def _flash_kernel(qt_ref, kn_ref, kd_ref, vt_ref, o_ref, s_ref, m_ref, acc_ref, *,
                  tk, nk, unroll, hps):
    dq = qt_ref.shape[0] // hps
    dv = vt_ref.shape[0] // hps
    dn = kn_ref.shape[1] // hps
    ones = jnp.ones((ONES_ROWS, tk), BF16)

    def scores(hd, c):
        off = pl.multiple_of(c * tk, tk)
        kcat = jnp.concatenate([kn_ref[pl.ds(off, tk), hd * dn:(hd + 1) * dn],
                                kd_ref[pl.ds(off, tk), :]], axis=1)
        return jnp.dot(kcat, qt_ref[hd * dq:(hd + 1) * dq, :],
                       preferred_element_type=F32)

    def update(hd, c, s):
        off = pl.multiple_of(c * tk, tk)
        m = m_ref[hd]
        m_new = jnp.maximum(m, jnp.max(s, axis=0, keepdims=True))
        alpha = jnp.exp2(m - m_new)
        p = jnp.exp2((s - m_new).astype(BF16))
        m_ref[hd] = m_new
        lhs = jnp.concatenate([vt_ref[hd * dv:(hd + 1) * dv, pl.ds(off, tk)], ones], axis=0)
        acc_ref[hd] = alpha * acc_ref[hd] + jnp.dot(lhs, p, preferred_element_type=F32)

    m_ref[...] = jnp.full(m_ref.shape, -jnp.inf, F32)
    acc_ref[...] = jnp.zeros(acc_ref.shape, F32)
    for hd in range(hps):
        s_ref[hd] = scores(hd, 0)

    def trip(c, last):
        s_cur = [s_ref[hd] for hd in range(hps)]
        for u in range(unroll):
            for hd in range(hps):
                s_next = None if (last and u == unroll - 1) else scores(hd, c + u + 1)
                update(hd, c + u, s_cur[hd])
                s_cur[hd] = s_next
        if not last:
            for hd in range(hps):
                s_ref[hd] = s_cur[hd]

    def body(ci, carry):
        trip(ci * unroll, False)
        return carry

    lax.fori_loop(0, nk // unroll - 1, body, 0)
    trip(nk - unroll, True)
    for hd in range(hps):
        o_ref[:, hd * dv:(hd + 1) * dv] = (
            acc_ref[hd, 0:dv, :] / acc_ref[hd, dv:dv + 1, :]).T.astype(o_ref.dtype)


def _flash(qt, kn, kdup, vt, *, bsz, seq, cfg):
    n = qt.shape[1]
    h = cfg.n_heads
    tq, tk = min(cfg.tq, seq), min(cfg.tk, seq)
    nq = seq // tq
    dq = qt.shape[0] // h
    dn, dv = cfg.qk_nope, cfg.v_head
    assert dn == LANES and dv == LANES and dq == 2 * LANES
    nk = seq // tk
    unroll = math.gcd(nk, cfg.flash_unroll)
    hps = math.gcd(h, cfg.flash_heads)
    return pl.pallas_call(
        functools.partial(_flash_kernel, tk=tk, nk=nk, unroll=unroll, hps=hps),
        grid=(bsz, h // hps, nq),
        in_specs=[
            pl.BlockSpec((hps * dq, tq), lambda b, hh, i: (hh, b * nq + i)),
            pl.BlockSpec((seq, hps * dn), lambda b, hh, i: (b, hh)),
            pl.BlockSpec((seq, LANES), lambda b, hh, i: (b, 0)),
            pl.BlockSpec((hps * dv, seq), lambda b, hh, i: (hh, b)),
        ],
        out_specs=pl.BlockSpec((tq, hps * dv), lambda b, hh, i: (b * nq + i, hh)),
        out_shape=jax.ShapeDtypeStruct((n, h * dv), BF16),
        scratch_shapes=[pltpu.VMEM((hps, tk, tq), F32), pltpu.VMEM((hps, 1, tq), F32),
                        pltpu.VMEM((hps, dv + ONES_ROWS, tq), F32)],
        compiler_params=_cp(("parallel", "parallel", "arbitrary"), 48),
        name="flash",
    )(qt, kn, kdup, vt)


def _out_kernel(x_ref, m_ref, w_ref, nw_ref, h_ref, hn_ref, *, eps):
    hh = x_ref[...] + jnp.dot(m_ref[...], w_ref[...], preferred_element_type=F32)
    h_ref[...] = hh
    ms = jnp.mean(hh * hh, axis=-1, keepdims=True)
    hn_ref[...] = (hh * lax.rsqrt(ms + eps) * nw_ref[...]).astype(hn_ref.dtype)


def _out_proj(x, merged, w_out, norm_w, *, cfg):
    n, dm = x.shape
    tm = cfg.tm_out
    return pl.pallas_call(
        functools.partial(_out_kernel, eps=cfg.eps),
        grid=(n // tm,),
        in_specs=[
            pl.BlockSpec((tm, dm), lambda i: (i, 0)),
            pl.BlockSpec((tm, dm), lambda i: (i, 0)),
            pl.BlockSpec((dm, dm), lambda i: (0, 0)),
            pl.BlockSpec((1, dm), lambda i: (0, 0)),
        ],
        out_specs=[pl.BlockSpec((tm, dm), lambda i: (i, 0)),
                   pl.BlockSpec((tm, dm), lambda i: (i, 0))],
        out_shape=[jax.ShapeDtypeStruct((n, dm), F32), jax.ShapeDtypeStruct((n, dm), BF16)],
        compiler_params=_cp(("parallel",), 48),
        name="out_proj",
    )(x, merged, w_out, norm_w)


def _router_kernel(hn_ref, wt_ref, b_ref, idx_ref, gate_ref, rank_ref, cnt_ref, base_ref, *,
                   ne, topk):
    @pl.when(pl.program_id(0) == 0)
    def _():
        base_ref[...] = jnp.zeros_like(base_ref)

    tm = hn_ref.shape[0]
    logits = lax.dot_general(wt_ref[...], hn_ref[...], (((1,), (1,)), ((), ())),
                             preferred_element_type=F32) + b_ref[...]
    eio = lax.broadcasted_iota(jnp.int32, (ne, tm), 0)
    work = logits
    vals, onehots = [], []
    for k in range(topk):
        mx = jnp.max(work, axis=0, keepdims=True)
        sel = jnp.min(jnp.where(work == mx, eio, ne), axis=0, keepdims=True)
        oh = eio == sel
        vals.append(mx)
        onehots.append(oh)
        idx_ref[k:k + 1, :] = sel
        work = jnp.where(oh, -jnp.inf, work)
    es = [jnp.exp(v - vals[0]) for v in vals]
    den = es[0]
    for e in es[1:]:
        den = den + e
    for k in range(topk):
        gate_ref[k:k + 1, :] = es[k] / den
    a = onehots[0].astype(F32)
    for oh in onehots[1:]:
        a = a + oh.astype(F32)
    r = lax.broadcasted_iota(jnp.int32, (tm, tm), 0)
    c = lax.broadcasted_iota(jnp.int32, (tm, tm), 1)
    su = (r < c).astype(BF16)
    cum = jnp.dot(a.astype(BF16), su, preferred_element_type=F32) + base_ref[:, 0:1]
    for k in range(topk):
        rk = jnp.sum(jnp.where(onehots[k], cum, 0.0), axis=0, keepdims=True)
        rank_ref[k:k + 1, :] = rk.astype(jnp.int32)
    base_ref[...] = base_ref[...] + jnp.sum(a, axis=1, keepdims=True)
    cnt_ref[...] = base_ref[...].astype(jnp.int32)


def _router(hn, w_router_t, b_router, *, cfg):
    n, dm = hn.shape
    ne, topk = cfg.n_experts, cfg.top_k
    tm = cfg.tm_route
    assert n % tm == 0
    return pl.pallas_call(
        functools.partial(_router_kernel, ne=ne, topk=topk),
        grid=(n // tm,),
        in_specs=[
            pl.BlockSpec((tm, dm), lambda i: (i, 0)),
            pl.BlockSpec((ne, dm), lambda i: (0, 0)),
            pl.BlockSpec((ne, 1), lambda i: (0, 0)),
        ],
        out_specs=[
            pl.BlockSpec((topk, tm), lambda i: (0, i)),
            pl.BlockSpec((topk, tm), lambda i: (0, i)),
            pl.BlockSpec((topk, tm), lambda i: (0, i)),
            pl.BlockSpec((ne, LANES), lambda i: (0, 0)),
        ],
        out_shape=[
            jax.ShapeDtypeStruct((topk, n), jnp.int32),
            jax.ShapeDtypeStruct((topk, n), F32),
            jax.ShapeDtypeStruct((topk, n), jnp.int32),
            jax.ShapeDtypeStruct((ne, LANES), jnp.int32),
        ],
        scratch_shapes=[pltpu.VMEM((ne, LANES), F32)],
        compiler_params=_cp(("arbitrary",)),
        name="router",
    )(hn, w_router_t, b_router)


def _expert_kernel(be_ref, nb_ref, x_ref, wg_ref, wu_ref, bg_ref, bu_ref, wda_ref, wdb_ref, bd_ref,
                   o_ref, acc_ref, *, limit, alpha, nj):
    i = pl.program_id(0)
    j = pl.program_id(1)

    @pl.when(i < nb_ref[0])
    def _():
        x = x_ref[...]
        g = jnp.dot(x, wg_ref[...], preferred_element_type=F32) + bg_ref[...]
        u = jnp.dot(x, wu_ref[...], preferred_element_type=F32) + bu_ref[...]
        g = jnp.minimum(g, limit)
        u = jnp.clip(u, -limit, limit)
        hh = (g * jax.nn.sigmoid(alpha * g) * (u + 1.0)).astype(BF16)
        part = jnp.concatenate([jnp.dot(hh, wda_ref[...], preferred_element_type=F32),
                                jnp.dot(hh, wdb_ref[...], preferred_element_type=F32)], axis=1)

        @pl.when(j == 0)
        def _():
            acc_ref[...] = part + bd_ref[...]

        @pl.when(j > 0)
        def _():
            acc_ref[...] = acc_ref[...] + part

    @pl.when(j == nj - 1)
    def _():
        o_ref[...] = acc_ref[...].astype(o_ref.dtype)


def _experts(xb, blk_e, n_used, w_g, w_u, b_gu, wd_a, wd_b, b_down, *, cfg):
    rows, dm = xb.shape
    dff = cfg.d_ff
    tm, tf = cfg.tm_moe, min(cfg.tf_moe, dff)
    nj = dff // tf
    nblk = rows // tm
    dh = dm // 2
    grid_spec = pltpu.PrefetchScalarGridSpec(
        num_scalar_prefetch=2,
        grid=(nblk, nj),
        in_specs=[
            pl.BlockSpec((tm, dm), lambda i, j, be, nb: (i, 0)),
            pl.BlockSpec((None, dm, tf), lambda i, j, be, nb: (be[i], 0, j)),
            pl.BlockSpec((None, dm, tf), lambda i, j, be, nb: (be[i], 0, j)),
            pl.BlockSpec((None, 1, tf), lambda i, j, be, nb: (be[i], 0, j)),
            pl.BlockSpec((None, 1, tf), lambda i, j, be, nb: (be[i], 0, nj + j)),
            pl.BlockSpec((None, tf, dh), lambda i, j, be, nb: (be[i], j, 0)),
            pl.BlockSpec((None, tf, dh), lambda i, j, be, nb: (be[i], j, 0)),
            pl.BlockSpec((None, 1, dm), lambda i, j, be, nb: (be[i], 0, 0)),
        ],
        out_specs=pl.BlockSpec((tm, dm), lambda i, j, be, nb: (i, 0)),
        scratch_shapes=[pltpu.VMEM((tm, dm), F32)],
    )
    return pl.pallas_call(
        functools.partial(_expert_kernel, limit=cfg.limit, alpha=cfg.alpha, nj=nj),
        grid_spec=grid_spec,
        out_shape=jax.ShapeDtypeStruct((rows, dm), BF16),
        compiler_params=_cp(("arbitrary", "arbitrary"), 48),
        name="experts",
    )(blk_e, n_used, xb, w_g, w_u, b_gu, b_gu, wd_a, wd_b, b_down)


def _combine_kernel(h_ref, y_ref, g_ref, nw_ref, o_ref, *, eps, topk):
    hh = h_ref[...]
    g = g_ref[...]
    for k in range(topk):
        hh = hh + g[:, k:k + 1] * y_ref[k].astype(F32)
    ms = jnp.mean(hh * hh, axis=-1, keepdims=True)
    o_ref[...] = hh * lax.rsqrt(ms + eps) * nw_ref[...]


def _combine(h, yk, gates_nk, norm_w, *, cfg):
    n, dm = h.shape
    topk = cfg.top_k
    tm = cfg.tm_comb
    return pl.pallas_call(
        functools.partial(_combine_kernel, eps=cfg.eps, topk=topk),
        grid=(n // tm,),
        in_specs=[
            pl.BlockSpec((tm, dm), lambda i: (i, 0)),
            pl.BlockSpec((topk, tm, dm), lambda i: (0, i, 0)),
            pl.BlockSpec((tm, topk), lambda i: (i, 0)),
            pl.BlockSpec((1, dm), lambda i: (0, 0)),
        ],
        out_specs=pl.BlockSpec((tm, dm), lambda i: (i, 0)),
        out_shape=jax.ShapeDtypeStruct((n, dm), F32),
        compiler_params=_cp(("parallel",), 48),
        name="combine",
    )(h, yk, gates_nk, norm_w)


def _cast_kernel(xa_ref, xb_ref, oa_ref, ob_ref):
    oa_ref[...] = xa_ref[...].astype(oa_ref.dtype)
    ob_ref[...] = xb_ref[...].astype(ob_ref.dtype)


def _cast_split_bf16(w, *, block_bytes=4 * 1024 * 1024):
    e, r, c = w.shape
    ch = c // 2
    tr = max(8, min(r, block_bytes // (ch * 4)))
    assert r % tr == 0 and ch % LANES == 0
    half = jax.ShapeDtypeStruct((e, r, ch), BF16)
    return pl.pallas_call(
        _cast_kernel,
        grid=(e, r // tr),
        in_specs=[pl.BlockSpec((None, tr, ch), lambda i, j: (i, j, 0)),
                  pl.BlockSpec((None, tr, ch), lambda i, j: (i, j, 1))],
        out_specs=[pl.BlockSpec((None, tr, ch), lambda i, j: (i, j, 0)),
                   pl.BlockSpec((None, tr, ch), lambda i, j: (i, j, 0))],
        out_shape=[half, half],
        compiler_params=_cp(("parallel", "parallel"), 40),
        name="cast_bf16",
    )(w, w)


def _rot_cols(w):
    half = w.shape[-1] // 2
    return jnp.concatenate([-w[..., half:], w[..., :half]], axis=-1)


def _prep(cfg, norm_mix_w, w_in, q_norm_w, kv_norm_w, w_uq, w_ukv, conv_w, conv_b, dt_bias_f,
          dt_bias_b, a_log_f, a_log_b, d_skip, ssm_norm_w, w_br_attn, w_br_ssm, w_out,
          norm_ffn_w, w_router, b_router, w_gu, b_gu, w_down, b_down, norm_final_w):
    c = cfg
    hs = c.d_inner // c.ssm_head_dim
    conv_dim = c.d_inner + 2 * c.n_groups * c.d_state
    off_kv = c.q_lora
    off_kr = off_kv + c.kv_lora
    off_z = off_kr + c.qk_rope
    off_xbc = off_z + c.d_inner
    off_dtf = off_xbc + conv_dim
    off_dtb = off_dtf + hs
    off_ga = off_dtb + hs
    off_gs = off_ga + c.d_model
    w = w_in
    w_kr = w[:, off_kr:off_z]
    w_a = jnp.concatenate([w[:, :off_kr], w_kr, _rot_cols(w_kr), w[:, off_dtf:off_ga]], axis=1)
    w_b = jnp.concatenate([w[:, off_z:off_dtf], w[:, off_ga:]], axis=1)
    h = c.n_heads
    qk = c.qk_nope + c.qk_rope
    wq = w_uq.reshape(c.q_lora, h, qk)
    wq_r = wq[..., c.qk_nope:]
    wq2 = jnp.concatenate([wq[..., :c.qk_nope], wq_r, _rot_cols(wq_r)], axis=-1)
    wq2 = wq2.reshape(c.q_lora, h * (c.qk_nope + 2 * c.qk_rope))
    wkv = w_ukv.reshape(c.kv_lora, h, c.qk_nope + c.v_head)
    wk = wkv[..., :c.qk_nope].reshape(c.kv_lora, h * c.qk_nope)
    wvt = wkv[..., c.qk_nope:].reshape(c.kv_lora, h * c.v_head).T
    w_g, w_u = _cast_split_bf16(w_gu)
    wd_a, wd_b = _cast_split_bf16(w_down)
    return dict(
        w_g=w_g, w_u=w_u, wd_a=wd_a, wd_b=wd_b,
        norm_mix_w=norm_mix_w, w_a=w_a.astype(BF16), w_b=w_b.astype(BF16),
        q_norm_w=q_norm_w, kv_norm_w=kv_norm_w, wq2t=wq2.T.astype(BF16), wk=wk.astype(BF16),
        wvt=wvt.astype(BF16),
        conv_w=conv_w, conv_b=conv_b,
        dt_bias2=jnp.concatenate([dt_bias_f, dt_bias_b]).reshape(1, 2 * hs).astype(F32),
        a2=(-jnp.exp(jnp.concatenate([a_log_f, a_log_b]).astype(F32))).reshape(1, 2 * hs),
        dskip_x=jnp.repeat(d_skip.astype(F32), c.ssm_head_dim).reshape(1, c.d_inner),
        ssm_norm_w=ssm_norm_w.reshape(1, c.d_inner).astype(F32),
        w_br_attn=w_br_attn.astype(BF16), w_br_ssm=w_br_ssm.astype(BF16),
        w_out=w_out.astype(BF16), norm_ffn_w=norm_ffn_w.reshape(1, c.d_model).astype(F32),
        w_router_t=w_router.T.astype(BF16), b_router=b_router.reshape(c.n_experts, 1).astype(F32),
        b_gu=b_gu.reshape(c.n_experts, 1, 2 * c.d_ff).astype(F32),
        b_down=b_down.reshape(c.n_experts, 1, c.d_model).astype(F32),
        norm_final_w=norm_final_w.reshape(1, c.d_model).astype(F32),
    )


def _rope_tabs(cfg, seq):
    half = cfg.qk_rope // 2
    inv_freq = 1.0 / (cfg.rope_theta ** (jnp.arange(0, cfg.qk_rope, 2, dtype=F32) / cfg.qk_rope))
    ang = jnp.arange(seq, dtype=F32)[:, None] * inv_freq[None, :]
    ang = jnp.concatenate([ang, ang], axis=-1)
    cos, sin = jnp.cos(ang), jnp.sin(ang)
    del half
    cs = jnp.concatenate([cos, sin], axis=1)
    qscale = (cfg.qk_nope + cfg.qk_rope) ** -0.5 * math.log2(math.e)
    qtab = jnp.concatenate([jnp.ones((seq, cfg.qk_nope), F32), cs], axis=1) * qscale
    return cs, qtab.T


def _moe_plan(idx, rank, counts, *, n, cfg):
    ne, topk, tm = cfg.n_experts, cfg.top_k, cfg.tm_moe
    nk = n * topk
    nblk = -(-(nk + ne * (tm - 1)) // tm)
    rows = nblk * tm
    padded = (counts + tm - 1) // tm * tm
    pad_ends = jnp.cumsum(padded)
    pad_starts = pad_ends - padded
    dest = pad_starts[idx] + rank
    tok = jnp.broadcast_to(jnp.arange(n, dtype=jnp.int32)[None, :], (topk, n))
    buf_tok = jnp.zeros((rows,), jnp.int32).at[dest.reshape(-1)].set(
        tok.reshape(-1), mode="promise_in_bounds", unique_indices=True)
    blk_start = jnp.arange(nblk, dtype=jnp.int32) * tm
    blk_e = jnp.minimum(jnp.sum((pad_ends[None, :] <= blk_start[:, None]).astype(jnp.int32), axis=1),
                        ne - 1)
    n_used = (pad_ends[-1] // tm).astype(jnp.int32).reshape(1)
    return dest, buf_tok, blk_e, n_used


def _trunk(x, p, cfg):
    c = cfg
    bsz, seq, dm = x.shape
    n = bsz * seq
    xf = x.reshape(n, dm)
    hs = c.d_inner // c.ssm_head_dim
    conv_dim = c.d_inner + 2 * c.n_groups * c.d_state
    tm = min(c.tm, seq)

    wa_cols = p["w_a"].shape[1]
    proj_a = _rms_matmul(xf, 0, dm, p["norm_mix_w"], p["w_a"], F32, tm=tm, tn=wa_cols, eps=c.eps,
                         name="in_proj_a")
    wb_cols = p["w_b"].shape[1]
    tn_b = 1024 if wb_cols % 1024 == 0 else 512
    proj_b = _rms_matmul(xf, 0, dm, p["norm_mix_w"], p["w_b"], BF16, tm=min(c.tm_in, seq),
                         tn=tn_b, eps=c.eps, name="in_proj_b", streams=2)
    col_kr = (c.q_lora + c.kv_lora) // LANES
    col_dt = col_kr + 1

    xbc = _conv_silu(proj_b, c.d_inner, p["conv_w"], p["conv_b"], bsz=bsz, seq=seq, cfg=c)
    y_f = _ssd_scan(xbc, proj_a, col_dt, p["dt_bias2"], p["a2"], reverse=False,
                    bsz=bsz, seq=seq, cfg=c)
    y_b = _ssd_scan(xbc, proj_a, col_dt, p["dt_bias2"], p["a2"], reverse=True,
                    bsz=bsz, seq=seq, cfg=c)

    cs, qtab = _rope_tabs(c, seq)
    assert c.q_lora == c.kv_lora
    qt = _rms_matmul_t(proj_a, 0, c.q_lora, p["q_norm_w"], p["wq2t"], BF16, tm=tm,
                       tn=p["wq2t"].shape[0], eps=c.eps, name="q_proj_t",
                       tab_t=qtab, seq=seq)
    kn = _rms_matmul(proj_a, 1, c.kv_lora, p["kv_norm_w"], p["wk"], BF16, tm=tm,
                     tn=p["wk"].shape[1], eps=c.eps, name="k_proj")
    vt = _rms_matmul_t(proj_a, 1, c.kv_lora, p["kv_norm_w"], p["wvt"], BF16, tm=tm,
                       tn=p["wvt"].shape[0], eps=c.eps, name="v_proj_t")
    kdup = _kdup(proj_a, col_kr, cs, seq=seq, tm=tm)
    attn = _flash(qt, kn, kdup, vt, bsz=bsz, seq=seq, cfg=c)
    ga_col0 = c.d_inner + conv_dim
    oa = _gated_matmul(attn, p["w_br_attn"], proj_b, ga_col0, tm=tm, tn=dm)

    merged = _ssm_out(y_f, y_b, xbc, proj_b, ga_col0 + dm, p["dskip_x"], p["ssm_norm_w"],
                      p["w_br_ssm"], oa, cfg=c)
    h, hn = _out_proj(xf, merged, p["w_out"], p["norm_ffn_w"], cfg=c)

    idx, gates, rank, cnt = _router(hn, p["w_router_t"], p["b_router"], cfg=c)
    dest, buf_tok, blk_e, n_used = _moe_plan(idx, rank, cnt[:, 0], n=n, cfg=c)
    xb = hn.at[buf_tok].get(mode="promise_in_bounds")
    yb = _experts(xb, blk_e, n_used, p["w_g"], p["w_u"], p["b_gu"], p["wd_a"], p["wd_b"],
                  p["b_down"], cfg=c)
    yk = yb.at[dest.reshape(-1)].get(mode="promise_in_bounds").reshape(c.top_k, n, dm)
    out = _combine(h, yk, gates.T, p["norm_final_w"], cfg=c)
    del hs
    return out.reshape(bsz, seq, dm)


def _forward(cfg, x_prompt, x_sample, *weights):
    depth = weights[0].shape[0]
    assert depth == 1
    names_per_layer = [w[0] for w in weights[:-1]]
    p = _prep(cfg, *names_per_layer, weights[-1])
    return _trunk(x_prompt, p, cfg), _trunk(x_sample, p, cfg)


def kernel(x_prompt, x_sample, norm_mix_w, w_in, q_norm_w, kv_norm_w, w_uq, w_ukv, conv_w, conv_b,
           dt_bias_f, dt_bias_b, a_log_f, a_log_b, d_skip, ssm_norm_w, w_br_attn, w_br_ssm,
           w_out, norm_ffn_w, w_router, b_router, w_gu, b_gu, w_down, b_down, norm_final_w):
    return _forward(Cfg(), x_prompt, x_sample, norm_mix_w, w_in, q_norm_w, kv_norm_w, w_uq, w_ukv,
                    conv_w, conv_b, dt_bias_f, dt_bias_b, a_log_f, a_log_b, d_skip, ssm_norm_w,
                    w_br_attn, w_br_ssm, w_out, norm_ffn_w, w_router, b_router, w_gu, b_gu,
                    w_down, b_down, norm_final_w)
```

```python
import functools
import math
from typing import NamedTuple

import jax
import jax.numpy as jnp
from jax import lax
from jax.experimental import pallas as pl
from jax.experimental.pallas import tpu as pltpu

F32 = jnp.float32
BF16 = jnp.bfloat16
LANES = 128
HALF = 64


class Cfg(NamedTuple):
    d_model: int = 2048
    n_heads: int = 16
    q_lora: int = 512
    kv_lora: int = 512
    qk_nope: int = 128
    qk_rope: int = 64
    v_head: int = 128
    rope_theta: float = 10000.0
    d_inner: int = 4096
    ssm_head_dim: int = 64
    n_groups: int = 8
    d_state: int = 128
    conv_k: int = 5
    chunk: int = 128
    n_experts: int = 32
    top_k: int = 4
    d_ff: int = 2048
    limit: float = 7.0
    alpha: float = 1.702
    eps: float = 1e-6
    tm: int = 512
    tm_in: int = 1024
    tq: int = 512
    tk: int = 512
    flash_unroll: int = 8
    flash_heads: int = 1
    t_conv: int = 512
    tc_conv: int = 2048
    tm_gate: int = 256
    tm_out: int = 256
    tm_route: int = 512
    tm_moe: int = 512
    tf_moe: int = 1024
    tm_comb: int = 256


def _cp(sem, vmem_mb=None):
    kw = dict(dimension_semantics=sem)
    if vmem_mb is not None:
        kw["vmem_limit_bytes"] = vmem_mb * 1024 * 1024
    return pltpu.CompilerParams(**kw)


def _rms_mm_kernel(x_ref, nw_ref, *rest, eps, streams):
    w_refs, (o_ref, xn_ref) = rest[:streams], rest[streams:]

    @pl.when(pl.program_id(1) == 0)
    def _():
        x = x_ref[...].astype(F32)
        ms = jnp.mean(x * x, axis=-1, keepdims=True)
        xn_ref[...] = (x * lax.rsqrt(ms + eps) * nw_ref[...]).astype(BF16)

    tw = o_ref.shape[1] // streams
    for s, w_ref in enumerate(w_refs):
        acc = jnp.dot(xn_ref[...], w_ref[...], preferred_element_type=F32)
        o_ref[:, s * tw:(s + 1) * tw] = acc.astype(o_ref.dtype)


def _rms_matmul(x, x_colblk, k, norm_w, w, out_dtype, *, tm, tn, eps, name, streams=1):
    n = x.shape[0]
    nout = w.shape[1]
    assert n % tm == 0 and nout % tn == 0 and w.shape[0] == k and tn % (streams * LANES) == 0
    tw = tn // streams
    in_specs = [
        pl.BlockSpec((tm, k), lambda i, j: (i, x_colblk)),
        pl.BlockSpec((1, k), lambda i, j: (0, 0)),
    ]
    for s in range(streams):
        in_specs.append(pl.BlockSpec((k, tw), lambda i, j, s=s: (0, j * streams + s)))
    return pl.pallas_call(
        functools.partial(_rms_mm_kernel, eps=eps, streams=streams),
        grid=(n // tm, nout // tn),
        in_specs=in_specs,
        out_specs=pl.BlockSpec((tm, tn), lambda i, j: (i, j)),
        out_shape=jax.ShapeDtypeStruct((n, nout), out_dtype),
        scratch_shapes=[pltpu.VMEM((tm, k), BF16)],
        compiler_params=_cp(("parallel", "arbitrary"), 48),
        name=name,
    )(x, norm_w.reshape(1, k).astype(F32), *([w] * streams))


def _rms_mm_t_kernel(x_ref, nw_ref, wt_ref, *rest, eps, tab_reps):
    if tab_reps:
        tab_ref, o_ref, xn_ref = rest
    else:
        o_ref, xn_ref = rest

    @pl.when(pl.program_id(1) == 0)
    def _():
        x = x_ref[...].astype(F32)
        ms = jnp.mean(x * x, axis=-1, keepdims=True)
        xn_ref[...] = (x * lax.rsqrt(ms + eps) * nw_ref[...]).astype(BF16)

    acc = lax.dot_general(wt_ref[...], xn_ref[...], (((1,), (1,)), ((), ())),
                          preferred_element_type=F32)
    if tab_reps:
        t = tab_ref[...]
        if tab_reps > 1:
            t = jnp.concatenate([t] * tab_reps, axis=0)
        acc = acc * t
    o_ref[...] = acc.astype(o_ref.dtype)


def _rms_matmul_t(x, x_colblk, k, norm_w, wt, out_dtype, *, tm, tn, eps, name, tab_t=None,
                  seq=None):
    n = x.shape[0]
    nout = wt.shape[0]
    assert n % tm == 0 and nout % tn == 0 and wt.shape[1] == k
    in_specs = [
        pl.BlockSpec((tm, k), lambda i, j: (i, x_colblk)),
        pl.BlockSpec((1, k), lambda i, j: (0, 0)),
        pl.BlockSpec((tn, k), lambda i, j: (j, 0)),
    ]
    args = [x, norm_w.reshape(1, k).astype(F32), wt]
    tab_reps = 0
    if tab_t is not None:
        tw = tab_t.shape[0]
        assert tn % tw == 0 and seq % tm == 0
        tab_reps = tn // tw
        nst = seq // tm
        in_specs.append(pl.BlockSpec((tw, tm), lambda i, j: (0, i % nst)))
        args.append(tab_t)
    return pl.pallas_call(
        functools.partial(_rms_mm_t_kernel, eps=eps, tab_reps=tab_reps),
        grid=(n // tm, nout // tn),
        in_specs=in_specs,
        out_specs=pl.BlockSpec((tn, tm), lambda i, j: (j, i)),
        out_shape=jax.ShapeDtypeStruct((nout, n), out_dtype),
        scratch_shapes=[pltpu.VMEM((tm, k), BF16)],
        compiler_params=_cp(("parallel", "arbitrary"), 48),
        name=name,
    )(*args)


HALO = 16


def _conv_kernel(prev_ref, x_ref, next_ref, w_ref, b_ref, o_ref, buf_ref, *, t, nt, kk):
    ti = pl.program_id(1)
    pv = prev_ref[...].astype(F32)[HALO - 8:, :]
    nx = next_ref[...].astype(F32)[:8, :]
    pv = jnp.where(ti == 0, 0.0, pv)
    nx = jnp.where(ti == nt - 1, 0.0, nx)
    buf_ref[0:8, :] = pv
    buf_ref[8:t + 8, :] = x_ref[...].astype(F32)
    buf_ref[t + 8:t + 16, :] = nx
    acc = b_ref[...]
    half = kk // 2
    for k in range(kk):
        acc = acc + w_ref[k:k + 1, :] * buf_ref[pl.ds(8 - half + k, t), :]
    o_ref[...] = (acc * jax.nn.sigmoid(acc)).astype(o_ref.dtype)


def _conv_silu(proj_b, col0, conv_w, conv_b, *, bsz, seq, cfg):
    n = proj_b.shape[0]
    cdim = conv_w.shape[1]
    t, tc = min(cfg.t_conv, seq), cfg.tc_conv
    assert seq % t == 0 and cdim % tc == 0 and col0 % tc == 0 and t % HALO == 0
    nt = seq // t
    cb0 = col0 // tc
    nrb = n // HALO
    rb = t // HALO

    def prev_map(b, ti, j):
        return (jnp.maximum((b * nt + ti) * rb - 1, 0), cb0 + j)

    def next_map(b, ti, j):
        return (jnp.minimum((b * nt + ti + 1) * rb, nrb - 1), cb0 + j)

    return pl.pallas_call(
        functools.partial(_conv_kernel, t=t, nt=nt, kk=cfg.conv_k),
        grid=(bsz, nt, cdim // tc),
        in_specs=[
            pl.BlockSpec((HALO, tc), prev_map),
            pl.BlockSpec((t, tc), lambda b, ti, j: (b * nt + ti, cb0 + j)),
            pl.BlockSpec((HALO, tc), next_map),
            pl.BlockSpec((cfg.conv_k, tc), lambda b, ti, j: (0, j)),
            pl.BlockSpec((1, tc), lambda b, ti, j: (0, j)),
        ],
        out_specs=pl.BlockSpec((t, tc), lambda b, ti, j: (b * nt + ti, j)),
        out_shape=jax.ShapeDtypeStruct((n, cdim), BF16),
        scratch_shapes=[pltpu.VMEM((t + 16, tc), F32)],
        compiler_params=_cp(("parallel", "parallel", "parallel")),
        name="conv_silu",
    )(proj_b, proj_b, proj_b, conv_w.astype(F32), conv_b.reshape(1, cdim).astype(F32))


def _ssd_kernel(xs_ref, b_ref, c_ref, dt_ref, bias_ref, a_ref, exp_ref, y_ref, st_ref, *,
                reverse, ll, ng, rr, pp, nn):
    assert pp == HALF and rr % 2 == 0 and 2 * ng * rr == LANES
    ci = pl.program_id(1)

    @pl.when(ci == 0)
    def _():
        st_ref[...] = jnp.zeros_like(st_ref)

    dcol = ng * rr if reverse else 0
    z = dt_ref[...] + bias_ref[...]
    dt_all = jnp.maximum(z, 0.0) + jnp.log(1.0 + jnp.exp(-jnp.abs(z)))
    dta = dt_all * a_ref[...]
    row = lax.broadcasted_iota(jnp.int32, (ll, ll), 0)
    col = lax.broadcasted_iota(jnp.int32, (ll, ll), 1)
    mask = (row <= col) if reverse else (row >= col)
    tri = mask.astype(F32)
    acum = jnp.dot(tri, dta, preferred_element_type=F32, precision=lax.Precision.HIGHEST)
    total = acum[0:1, :] if reverse else acum[ll - 1:ll, :]
    eac = jnp.exp(acum)
    wend = jnp.exp(total - acum) * dt_all
    etot = jnp.exp(total)
    acum_t = acum.T
    dt_t = dt_all.T
    lane = lax.broadcasted_iota(jnp.int32, (ll, LANES), 1)
    lo = lane < HALF
    v = jnp.concatenate([eac, wend, jnp.broadcast_to(etot, (16, LANES))], axis=0)
    v_hi = v.astype(BF16)
    v_lo = (v - v_hi.astype(F32)).astype(BF16)
    ex = jnp.dot(jnp.concatenate([v_hi, v_lo], axis=1), exp_ref[...],
                 preferred_element_type=F32)

    for g in range(ng):
        bg = b_ref[:, g * nn:(g + 1) * nn]
        cg = c_ref[:, g * nn:(g + 1) * nn]
        cb = lax.dot_general(cg, bg, (((1,), (1,)), ((), ())), preferred_element_type=F32)
        st = st_ref[g]
        gw = rr * pp
        yint = jnp.dot(cg, st.astype(BF16), preferred_element_type=F32)
        xw_parts = []
        for pr in range(rr // 2):
            c0 = dcol + g * rr + 2 * pr
            lanes0 = g * gw + pr * LANES
            xp = xs_ref[:, lanes0:lanes0 + LANES]
            ys = []
            for c in (c0, c0 + 1):
                seg = acum[:, c:c + 1] - acum_t[c:c + 1, :]
                dec = jnp.exp(jnp.where(mask, seg, -jnp.inf))
                m = (cb * dec * dt_t[c:c + 1, :]).astype(BF16)
                ys.append(jnp.dot(m, xp, preferred_element_type=F32))
            e_sel = ex[0:ll, lanes0:lanes0 + LANES]
            y = jnp.where(lo, ys[0], ys[1]) + yint[:, pr * LANES:(pr + 1) * LANES] * e_sel
            y_ref[:, lanes0:lanes0 + LANES] = y.astype(y_ref.dtype)
            w_sel = ex[ll:2 * ll, lanes0:lanes0 + LANES]
            xw_parts.append((xp.astype(F32) * w_sel).astype(BF16))
        xw = jnp.concatenate(xw_parts, axis=1)
        et = ex[2 * ll:2 * ll + 1, g * gw:(g + 1) * gw]
        upd = lax.dot_general(bg, xw, (((0,), (0,)), ((), ())), preferred_element_type=F32)
        st_ref[g] = st * et + upd


def _ssd_scan(xbc, dt_src, dt_colblk, bias2, a2, *, reverse, bsz, seq, cfg):
    n = xbc.shape[0]
    ll, ng, pp, nn = cfg.chunk, cfg.n_groups, cfg.ssm_head_dim, cfg.d_state
    di = cfg.d_inner
    rr = di // pp // ng
    nc = seq // ll
    gn = ng * nn
    assert di % gn == 0
    boff = di // gn

    def rblk(b, c):
        return b * nc + ((nc - 1 - c) if reverse else c)

    dcol = ng * rr if reverse else 0
    head_of_lane = jnp.arange(di, dtype=jnp.int32) // pp + dcol
    expand = (jnp.arange(LANES, dtype=jnp.int32)[:, None] == head_of_lane[None, :]).astype(BF16)
    expand2 = jnp.concatenate([expand, expand], axis=0)

    return pl.pallas_call(
        functools.partial(_ssd_kernel, reverse=reverse, ll=ll, ng=ng, rr=rr, pp=pp, nn=nn),
        grid=(bsz, nc),
        in_specs=[
            pl.BlockSpec((ll, di), lambda b, c: (rblk(b, c), 0)),
            pl.BlockSpec((ll, gn), lambda b, c: (rblk(b, c), boff)),
            pl.BlockSpec((ll, gn), lambda b, c: (rblk(b, c), boff + 1)),
            pl.BlockSpec((ll, LANES), lambda b, c: (rblk(b, c), dt_colblk)),
            pl.BlockSpec((1, LANES), lambda b, c: (0, 0)),
            pl.BlockSpec((1, LANES), lambda b, c: (0, 0)),
            pl.BlockSpec((2 * LANES, di), lambda b, c: (0, 0)),
        ],
        out_specs=pl.BlockSpec((ll, di), lambda b, c: (rblk(b, c), 0)),
        out_shape=jax.ShapeDtypeStruct((n, di), BF16),
        scratch_shapes=[pltpu.VMEM((ng, nn, rr * pp), F32)],
        compiler_params=_cp(("parallel", "arbitrary")),
        name="ssd_bwd" if reverse else "ssd_fwd",
    )(xbc, xbc, xbc, dt_src, bias2, a2, expand2)


def _ssm_out_kernel(yf_ref, yb_ref, xs_ref, z_ref, dsk_ref, nw_ref, w_ref, gs_ref, oa_ref,
                    o_ref, yn_ref, *, eps, ng):
    @pl.when(pl.program_id(1) == 0)
    def _():
        di = yn_ref.shape[1]
        gw = di // ng
        for g in range(ng):
            sl = slice(g * gw, (g + 1) * gw)
            y = (yf_ref[:, sl].astype(F32) + yb_ref[:, sl].astype(F32)
                 + dsk_ref[:, sl] * xs_ref[:, sl].astype(F32))
            zz = z_ref[:, sl].astype(F32)
            y = y * (zz * jax.nn.sigmoid(zz))
            ms = jnp.mean(y * y, axis=-1, keepdims=True)
            yn_ref[:, sl] = (y * lax.rsqrt(ms + eps) * nw_ref[:, sl]).astype(BF16)

    acc = jnp.dot(yn_ref[...], w_ref[...], preferred_element_type=F32)
    gate = jax.nn.sigmoid(gs_ref[...].astype(F32))
    o_ref[...] = (oa_ref[...].astype(F32) + gate * acc).astype(o_ref.dtype)


def _ssm_out(yf, yb, xbc, proj_b, gs_col0, dskip_x, norm_w, w_br, oa, *, cfg):
    n = yf.shape[0]
    di, dm = cfg.d_inner, cfg.d_model
    tm, tn = cfg.tm_gate, dm
    assert n % tm == 0 and dm % tn == 0 and gs_col0 % tn == 0
    gsb = gs_col0 // tn
    return pl.pallas_call(
        functools.partial(_ssm_out_kernel, eps=cfg.eps, ng=cfg.n_groups),
        grid=(n // tm, dm // tn),
        in_specs=[
            pl.BlockSpec((tm, di), lambda i, j: (i, 0)),
            pl.BlockSpec((tm, di), lambda i, j: (i, 0)),
            pl.BlockSpec((tm, di), lambda i, j: (i, 0)),
            pl.BlockSpec((tm, di), lambda i, j: (i, 0)),
            pl.BlockSpec((1, di), lambda i, j: (0, 0)),
            pl.BlockSpec((1, di), lambda i, j: (0, 0)),
            pl.BlockSpec((di, tn), lambda i, j: (0, j), pipeline_mode=pl.Buffered(1)),
            pl.BlockSpec((tm, tn), lambda i, j: (i, gsb + j)),
            pl.BlockSpec((tm, tn), lambda i, j: (i, j)),
        ],
        out_specs=pl.BlockSpec((tm, tn), lambda i, j: (i, j)),
        out_shape=jax.ShapeDtypeStruct((n, dm), BF16),
        scratch_shapes=[pltpu.VMEM((tm, di), BF16)],
        compiler_params=_cp(("parallel", "arbitrary"), 48),
        name="ssm_out",
    )(yf, yb, xbc, proj_b, dskip_x, norm_w, w_br, proj_b, oa)


def _gated_mm_kernel(x_ref, w_ref, g_ref, o_ref):
    acc = jnp.dot(x_ref[...], w_ref[...], preferred_element_type=F32)
    o_ref[...] = (jax.nn.sigmoid(g_ref[...].astype(F32)) * acc).astype(o_ref.dtype)


def _gated_matmul(x, w, gate_src, gate_col0, *, tm, tn):
    n, k = x.shape
    nout = w.shape[1]
    assert n % tm == 0 and nout % tn == 0 and gate_col0 % tn == 0
    gb = gate_col0 // tn
    return pl.pallas_call(
        _gated_mm_kernel,
        grid=(n // tm, nout // tn),
        in_specs=[
            pl.BlockSpec((tm, k), lambda i, j: (i, 0)),
            pl.BlockSpec((k, tn), lambda i, j: (0, j)),
            pl.BlockSpec((tm, tn), lambda i, j: (i, gb + j)),
        ],
        out_specs=pl.BlockSpec((tm, tn), lambda i, j: (i, j)),
        out_shape=jax.ShapeDtypeStruct((n, nout), BF16),
        compiler_params=_cp(("parallel", "arbitrary"), 48),
        name="attn_br",
    )(x, w, gate_src)


def _kdup_kernel(kr_ref, cs_ref, o_ref):
    k2 = kr_ref[...].astype(F32) * cs_ref[...]
    o_ref[...] = (k2 + pltpu.roll(k2, HALF, 1)).astype(o_ref.dtype)


def _kdup(proj_a, colblk, cs, *, seq, tm):
    n = proj_a.shape[0]
    tm = min(tm, seq)
    nst = seq // tm
    return pl.pallas_call(
        _kdup_kernel,
        grid=(n // tm,),
        in_specs=[
            pl.BlockSpec((tm, LANES), lambda i: (i, colblk)),
            pl.BlockSpec((tm, LANES), lambda i: (i % nst, 0)),
        ],
        out_specs=pl.BlockSpec((tm, LANES), lambda i: (i, 0)),
        out_shape=jax.ShapeDtypeStruct((n, LANES), BF16),
        compiler_params=_cp(("parallel",)),
        name="kdup",
    )(proj_a, cs)


ONES_ROWS = 16


def _flash_kernel(qt_ref, kn_ref, kd_ref, vt_ref, o_ref, s_ref, m_ref, acc_ref, *,
                  tk, nk, unroll, hps):
    dq = qt_ref.shape[0] // hps
    dv = vt_ref.shape[0] // hps
    dn = kn_ref.shape[1] // hps
    ones = jnp.ones((ONES_ROWS, tk), BF16)

    def scores(hd, c):
        off = pl.multiple_of(c * tk, tk)
        kcat = jnp.concatenate([kn_ref[pl.ds(off, tk), hd * dn:(hd + 1) * dn],
                                kd_ref[pl.ds(off, tk), :]], axis=1)
        return jnp.dot(kcat, qt_ref[hd * dq:(hd + 1) * dq, :],
                       preferred_element_type=F32)

    def update(hd, c, s):
        off = pl.multiple_of(c * tk, tk)
        m = m_ref[hd]
        m_new = jnp.maximum(m, jnp.max(s, axis=0, keepdims=True))
        alpha = jnp.exp2(m - m_new)
        p = jnp.exp2((s - m_new).astype(BF16))
        m_ref[hd] = m_new
        lhs = jnp.concatenate([vt_ref[hd * dv:(hd + 1) * dv, pl.ds(off, tk)], ones], axis=0)
        acc_ref[hd] = alpha * acc_ref[hd] + jnp.dot(lhs, p, preferred_element_type=F32)

    m_ref[...] = jnp.full(m_ref.shape, -jnp.inf, F32)
    acc_ref[...] = jnp.zeros(acc_ref.shape, F32)
    for hd in range(hps):
        s_ref[hd] = scores(hd, 0)

    def trip(c, last):
        s_cur = [s_ref[hd] for hd in range(hps)]
        for u in range(unroll):
            for hd in range(hps):
                s_next = None if (last and u == unroll - 1) else scores(hd, c + u + 1)
                update(hd, c + u, s_cur[hd])
                s_cur[hd] = s_next
        if not last:
            for hd in range(hps):
                s_ref[hd] = s_cur[hd]

    def body(ci, carry):
        trip(ci * unroll, False)
        return carry

    lax.fori_loop(0, nk // unroll - 1, body, 0)
    trip(nk - unroll, True)
    for hd in range(hps):
        o_ref[:, hd * dv:(hd + 1) * dv] = (
            acc_ref[hd, 0:dv, :] / acc_ref[hd, dv:dv + 1, :]).T.astype(o_ref.dtype)


def _flash(qt, kn, kdup, vt, *, bsz, seq, cfg):
    n = qt.shape[1]
    h = cfg.n_heads
    tq, tk = min(cfg.tq, seq), min(cfg.tk, seq)
    nq = seq // tq
    dq = qt.shape[0] // h
    dn, dv = cfg.qk_nope, cfg.v_head
    assert dn == LANES and dv == LANES and dq == 2 * LANES
    nk = seq // tk
    unroll = math.gcd(nk, cfg.flash_unroll)
    hps = math.gcd(h, cfg.flash_heads)
    return pl.pallas_call(
        functools.partial(_flash_kernel, tk=tk, nk=nk, unroll=unroll, hps=hps),
        grid=(bsz, h // hps, nq),
        in_specs=[
            pl.BlockSpec((hps * dq, tq), lambda b, hh, i: (hh, b * nq + i)),
            pl.BlockSpec((seq, hps * dn), lambda b, hh, i: (b, hh)),
            pl.BlockSpec((seq, LANES), lambda b, hh, i: (b, 0)),
            pl.BlockSpec((hps * dv, seq), lambda b, hh, i: (hh, b)),
        ],
        out_specs=pl.BlockSpec((tq, hps * dv), lambda b, hh, i: (b * nq + i, hh)),
        out_shape=jax.ShapeDtypeStruct((n, h * dv), BF16),
        scratch_shapes=[pltpu.VMEM((hps, tk, tq), F32), pltpu.VMEM((hps, 1, tq), F32),
                        pltpu.VMEM((hps, dv + ONES_ROWS, tq), F32)],
        compiler_params=_cp(("parallel", "parallel", "arbitrary"), 48),
        name="flash",
    )(qt, kn, kdup, vt)


def _out_kernel(x_ref, m_ref, w_ref, nw_ref, h_ref, hn_ref, *, eps):
    hh = x_ref[...] + jnp.dot(m_ref[...], w_ref[...], preferred_element_type=F32)
    h_ref[...] = hh
    ms = jnp.mean(hh * hh, axis=-1, keepdims=True)
    hn_ref[...] = (hh * lax.rsqrt(ms + eps) * nw_ref[...]).astype(hn_ref.dtype)


def _out_proj(x, merged, w_out, norm_w, *, cfg):
    n, dm = x.shape
    tm = cfg.tm_out
    return pl.pallas_call(
        functools.partial(_out_kernel, eps=cfg.eps),
        grid=(n // tm,),
        in_specs=[
            pl.BlockSpec((tm, dm), lambda i: (i, 0)),
            pl.BlockSpec((tm, dm), lambda i: (i, 0)),
            pl.BlockSpec((dm, dm), lambda i: (0, 0)),
            pl.BlockSpec((1, dm), lambda i: (0, 0)),
        ],
        out_specs=[pl.BlockSpec((tm, dm), lambda i: (i, 0)),
                   pl.BlockSpec((tm, dm), lambda i: (i, 0))],
        out_shape=[jax.ShapeDtypeStruct((n, dm), F32), jax.ShapeDtypeStruct((n, dm), BF16)],
        compiler_params=_cp(("parallel",), 48),
        name="out_proj",
    )(x, merged, w_out, norm_w)


def _router_kernel(hn_ref, wt_ref, b_ref, idx_ref, gate_ref, rank_ref, cnt_ref, base_ref, *,
                   ne, topk):
    @pl.when(pl.program_id(0) == 0)
    def _():
        base_ref[...] = jnp.zeros_like(base_ref)

    tm = hn_ref.shape[0]
    logits = lax.dot_general(wt_ref[...], hn_ref[...], (((1,), (1,)), ((), ())),
                             preferred_element_type=F32) + b_ref[...]
    eio = lax.broadcasted_iota(jnp.int32, (ne, tm), 0)
    work = logits
    vals, onehots = [], []
    for k in range(topk):
        mx = jnp.max(work, axis=0, keepdims=True)
        sel = jnp.min(jnp.where(work == mx, eio, ne), axis=0, keepdims=True)
        oh = eio == sel
        vals.append(mx)
        onehots.append(oh)
        idx_ref[k:k + 1, :] = sel
        work = jnp.where(oh, -jnp.inf, work)
    es = [jnp.exp(v - vals[0]) for v in vals]
    den = es[0]
    for e in es[1:]:
        den = den + e
    for k in range(topk):
        gate_ref[k:k + 1, :] = es[k] / den
    a = onehots[0].astype(F32)
    for oh in onehots[1:]:
        a = a + oh.astype(F32)
    r = lax.broadcasted_iota(jnp.int32, (tm, tm), 0)
    c = lax.broadcasted_iota(jnp.int32, (tm, tm), 1)
    su = (r < c).astype(BF16)
    cum = jnp.dot(a.astype(BF16), su, preferred_element_type=F32) + base_ref[:, 0:1]
    for k in range(topk):
        rk = jnp.sum(jnp.where(onehots[k], cum, 0.0), axis=0, keepdims=True)
        rank_ref[k:k + 1, :] = rk.astype(jnp.int32)
    base_ref[...] = base_ref[...] + jnp.sum(a, axis=1, keepdims=True)
    cnt_ref[...] = base_ref[...].astype(jnp.int32)


def _router(hn, w_router_t, b_router, *, cfg):
    n, dm = hn.shape
    ne, topk = cfg.n_experts, cfg.top_k
    tm = cfg.tm_route
    assert n % tm == 0
    return pl.pallas_call(
        functools.partial(_router_kernel, ne=ne, topk=topk),
        grid=(n // tm,),
        in_specs=[
            pl.BlockSpec((tm, dm), lambda i: (i, 0)),
            pl.BlockSpec((ne, dm), lambda i: (0, 0)),
            pl.BlockSpec((ne, 1), lambda i: (0, 0)),
        ],
        out_specs=[
            pl.BlockSpec((topk, tm), lambda i: (0, i)),
            pl.BlockSpec((topk, tm), lambda i: (0, i)),
            pl.BlockSpec((topk, tm), lambda i: (0, i)),
            pl.BlockSpec((ne, LANES), lambda i: (0, 0)),
        ],
        out_shape=[
            jax.ShapeDtypeStruct((topk, n), jnp.int32),
            jax.ShapeDtypeStruct((topk, n), F32),
            jax.ShapeDtypeStruct((topk, n), jnp.int32),
            jax.ShapeDtypeStruct((ne, LANES), jnp.int32),
        ],
        scratch_shapes=[pltpu.VMEM((ne, LANES), F32)],
        compiler_params=_cp(("arbitrary",)),
        name="router",
    )(hn, w_router_t, b_router)


def _expert_kernel(be_ref, nv_ref, x_ref, wg_ref, wu_ref, bg_ref, bu_ref, wda_ref, wdb_ref, bd_ref,
                   o_ref, acc_ref, *, limit, alpha, nj):
    i = pl.program_id(0)
    j = pl.program_id(1)
    nv = nv_ref[i]
    tm = x_ref.shape[0]
    half = tm // 2

    def mlp(rows):
        x = x_ref[0:rows, :]
        g = jnp.dot(x, wg_ref[...], preferred_element_type=F32) + bg_ref[...]
        u = jnp.dot(x, wu_ref[...], preferred_element_type=F32) + bu_ref[...]
        g = jnp.minimum(g, limit)
        u = jnp.clip(u, -limit, limit)
        hh = (g * jax.nn.sigmoid(alpha * g) * (u + 1.0)).astype(BF16)
        part = jnp.concatenate([jnp.dot(hh, wda_ref[...], preferred_element_type=F32),
                                jnp.dot(hh, wdb_ref[...], preferred_element_type=F32)], axis=1)

        @pl.when(j == 0)
        def _():
            acc_ref[0:rows, :] = part + bd_ref[...]

        @pl.when(j > 0)
        def _():
            acc_ref[0:rows, :] = acc_ref[0:rows, :] + part

    @pl.when(nv > half)
    def _():
        mlp(tm)

    @pl.when(jnp.logical_and(nv > 0, nv <= half))
    def _():
        mlp(half)

        @pl.when(j == 0)
        def _():
            acc_ref[half:tm, :] = jnp.zeros((tm - half, acc_ref.shape[1]), F32)

    @pl.when(j == nj - 1)
    def _():
        o_ref[...] = acc_ref[...].astype(o_ref.dtype)


def _experts(xb, blk_e, blk_valid, w_g, w_u, b_gu, wd_a, wd_b, b_down, *, cfg):
    rows, dm = xb.shape
    dff = cfg.d_ff
    tm, tf = cfg.tm_moe, min(cfg.tf_moe, dff)
    nj = dff // tf
    nblk = rows // tm
    dh = dm // 2
    grid_spec = pltpu.PrefetchScalarGridSpec(
        num_scalar_prefetch=2,
        grid=(nblk, nj),
        in_specs=[
            pl.BlockSpec((tm, dm), lambda i, j, be, nb: (i, 0)),
            pl.BlockSpec((None, dm, tf), lambda i, j, be, nb: (be[i], 0, j)),
            pl.BlockSpec((None, dm, tf), lambda i, j, be, nb: (be[i], 0, j)),
            pl.BlockSpec((None, 1, tf), lambda i, j, be, nb: (be[i], 0, j)),
            pl.BlockSpec((None, 1, tf), lambda i, j, be, nb: (be[i], 0, nj + j)),
            pl.BlockSpec((None, tf, dh), lambda i, j, be, nb: (be[i], j, 0)),
            pl.BlockSpec((None, tf, dh), lambda i, j, be, nb: (be[i], j, 0)),
            pl.BlockSpec((None, 1, dm), lambda i, j, be, nb: (be[i], 0, 0)),
        ],
        out_specs=pl.BlockSpec((tm, dm), lambda i, j, be, nb: (i, 0)),
        scratch_shapes=[pltpu.VMEM((tm, dm), F32)],
    )
    return pl.pallas_call(
        functools.partial(_expert_kernel, limit=cfg.limit, alpha=cfg.alpha, nj=nj),
        grid_spec=grid_spec,
        out_shape=jax.ShapeDtypeStruct((rows, dm), BF16),
        compiler_params=_cp(("arbitrary", "arbitrary"), 48),
        name="experts",
    )(blk_e, blk_valid, xb, w_g, w_u, b_gu, b_gu, wd_a, wd_b, b_down)


def _combine_kernel(h_ref, y_ref, g_ref, nw_ref, o_ref, *, eps, topk):
    hh = h_ref[...]
    g = g_ref[...]
    for k in range(topk):
        hh = hh + g[:, k:k + 1] * y_ref[k].astype(F32)
    ms = jnp.mean(hh * hh, axis=-1, keepdims=True)
    o_ref[...] = hh * lax.rsqrt(ms + eps) * nw_ref[...]


def _combine(h, yk, gates_nk, norm_w, *, cfg):
    n, dm = h.shape
    topk = cfg.top_k
    tm = cfg.tm_comb
    return pl.pallas_call(
        functools.partial(_combine_kernel, eps=cfg.eps, topk=topk),
        grid=(n // tm,),
        in_specs=[
            pl.BlockSpec((tm, dm), lambda i: (i, 0)),
            pl.BlockSpec((topk, tm, dm), lambda i: (0, i, 0)),
            pl.BlockSpec((tm, topk), lambda i: (i, 0)),
            pl.BlockSpec((1, dm), lambda i: (0, 0)),
        ],
        out_specs=pl.BlockSpec((tm, dm), lambda i: (i, 0)),
        out_shape=jax.ShapeDtypeStruct((n, dm), F32),
        compiler_params=_cp(("parallel",), 48),
        name="combine",
    )(h, yk, gates_nk, norm_w)


def _cast_kernel(xa_ref, xb_ref, oa_ref, ob_ref):
    oa_ref[...] = xa_ref[...].astype(oa_ref.dtype)
    ob_ref[...] = xb_ref[...].astype(ob_ref.dtype)


def _cast_split_bf16(w, *, block_bytes=4 * 1024 * 1024):
    e, r, c = w.shape
    ch = c // 2
    tr = max(8, min(r, block_bytes // (ch * 4)))
    assert r % tr == 0 and ch % LANES == 0
    half = jax.ShapeDtypeStruct((e, r, ch), BF16)
    return pl.pallas_call(
        _cast_kernel,
        grid=(e, r // tr),
        in_specs=[pl.BlockSpec((None, tr, ch), lambda i, j: (i, j, 0)),
                  pl.BlockSpec((None, tr, ch), lambda i, j: (i, j, 1))],
        out_specs=[pl.BlockSpec((None, tr, ch), lambda i, j: (i, j, 0)),
                   pl.BlockSpec((None, tr, ch), lambda i, j: (i, j, 0))],
        out_shape=[half, half],
        compiler_params=_cp(("parallel", "parallel"), 40),
        name="cast_bf16",
    )(w, w)


def _rot_cols(w):
    half = w.shape[-1] // 2
    return jnp.concatenate([-w[..., half:], w[..., :half]], axis=-1)


def _prep(cfg, norm_mix_w, w_in, q_norm_w, kv_norm_w, w_uq, w_ukv, conv_w, conv_b, dt_bias_f,
          dt_bias_b, a_log_f, a_log_b, d_skip, ssm_norm_w, w_br_attn, w_br_ssm, w_out,
          norm_ffn_w, w_router, b_router, w_gu, b_gu, w_down, b_down, norm_final_w):
    c = cfg
    hs = c.d_inner // c.ssm_head_dim
    conv_dim = c.d_inner + 2 * c.n_groups * c.d_state
    off_kv = c.q_lora
    off_kr = off_kv + c.kv_lora
    off_z = off_kr + c.qk_rope
    off_xbc = off_z + c.d_inner
    off_dtf = off_xbc + conv_dim
    off_dtb = off_dtf + hs
    off_ga = off_dtb + hs
    off_gs = off_ga + c.d_model
    w = w_in
    w_kr = w[:, off_kr:off_z]
    w_a = jnp.concatenate([w[:, :off_kr], w_kr, _rot_cols(w_kr), w[:, off_dtf:off_ga]], axis=1)
    w_b = jnp.concatenate([w[:, off_z:off_dtf], w[:, off_ga:]], axis=1)
    h = c.n_heads
    qk = c.qk_nope + c.qk_rope
    wq = w_uq.reshape(c.q_lora, h, qk)
    wq_r = wq[..., c.qk_nope:]
    wq2 = jnp.concatenate([wq[..., :c.qk_nope], wq_r, _rot_cols(wq_r)], axis=-1)
    wq2 = wq2.reshape(c.q_lora, h * (c.qk_nope + 2 * c.qk_rope))
    wkv = w_ukv.reshape(c.kv_lora, h, c.qk_nope + c.v_head)
    wk = wkv[..., :c.qk_nope].reshape(c.kv_lora, h * c.qk_nope)
    wvt = wkv[..., c.qk_nope:].reshape(c.kv_lora, h * c.v_head).T
    w_g, w_u = _cast_split_bf16(w_gu)
    wd_a, wd_b = _cast_split_bf16(w_down)
    return dict(
        w_g=w_g, w_u=w_u, wd_a=wd_a, wd_b=wd_b,
        norm_mix_w=norm_mix_w, w_a=w_a.astype(BF16), w_b=w_b.astype(BF16),
        q_norm_w=q_norm_w, kv_norm_w=kv_norm_w, wq2t=wq2.T.astype(BF16), wk=wk.astype(BF16),
        wvt=wvt.astype(BF16),
        conv_w=conv_w, conv_b=conv_b,
        dt_bias2=jnp.concatenate([dt_bias_f, dt_bias_b]).reshape(1, 2 * hs).astype(F32),
        a2=(-jnp.exp(jnp.concatenate([a_log_f, a_log_b]).astype(F32))).reshape(1, 2 * hs),
        dskip_x=jnp.repeat(d_skip.astype(F32), c.ssm_head_dim).reshape(1, c.d_inner),
        ssm_norm_w=ssm_norm_w.reshape(1, c.d_inner).astype(F32),
        w_br_attn=w_br_attn.astype(BF16), w_br_ssm=w_br_ssm.astype(BF16),
        w_out=w_out.astype(BF16), norm_ffn_w=norm_ffn_w.reshape(1, c.d_model).astype(F32),
        w_router_t=w_router.T.astype(BF16), b_router=b_router.reshape(c.n_experts, 1).astype(F32),
        b_gu=b_gu.reshape(c.n_experts, 1, 2 * c.d_ff).astype(F32),
        b_down=b_down.reshape(c.n_experts, 1, c.d_model).astype(F32),
        norm_final_w=norm_final_w.reshape(1, c.d_model).astype(F32),
    )


def _rope_tabs(cfg, seq):
    half = cfg.qk_rope // 2
    inv_freq = 1.0 / (cfg.rope_theta ** (jnp.arange(0, cfg.qk_rope, 2, dtype=F32) / cfg.qk_rope))
    ang = jnp.arange(seq, dtype=F32)[:, None] * inv_freq[None, :]
    ang = jnp.concatenate([ang, ang], axis=-1)
    cos, sin = jnp.cos(ang), jnp.sin(ang)
    del half
    cs = jnp.concatenate([cos, sin], axis=1)
    qscale = (cfg.qk_nope + cfg.qk_rope) ** -0.5 * math.log2(math.e)
    qtab = jnp.concatenate([jnp.ones((seq, cfg.qk_nope), F32), cs], axis=1) * qscale
    return cs, qtab.T


def _moe_plan(idx, rank, counts, *, n, cfg):
    ne, topk, tm = cfg.n_experts, cfg.top_k, cfg.tm_moe
    nk = n * topk
    nblk = -(-(nk + ne * (tm - 1)) // tm)
    rows = nblk * tm
    padded = (counts + tm - 1) // tm * tm
    pad_ends = jnp.cumsum(padded)
    pad_starts = pad_ends - padded
    dest = pad_starts[idx] + rank
    tok = jnp.broadcast_to(jnp.arange(n, dtype=jnp.int32)[None, :], (topk, n))
    buf_tok = jnp.zeros((rows,), jnp.int32).at[dest.reshape(-1)].set(
        tok.reshape(-1), mode="promise_in_bounds", unique_indices=True)
    blk_start = jnp.arange(nblk, dtype=jnp.int32) * tm
    blk_e = jnp.minimum(jnp.sum((pad_ends[None, :] <= blk_start[:, None]).astype(jnp.int32), axis=1),
                        ne - 1)
    blk_valid = jnp.clip(pad_starts[blk_e] + counts[blk_e] - blk_start, 0, tm).astype(jnp.int32)
    return dest, buf_tok, blk_e, blk_valid


def _trunk(x, p, cfg):
    c = cfg
    bsz, seq, dm = x.shape
    n = bsz * seq
    xf = x.reshape(n, dm)
    hs = c.d_inner // c.ssm_head_dim
    conv_dim = c.d_inner + 2 * c.n_groups * c.d_state
    tm = min(c.tm, seq)

    wa_cols = p["w_a"].shape[1]
    proj_a = _rms_matmul(xf, 0, dm, p["norm_mix_w"], p["w_a"], F32, tm=tm, tn=wa_cols, eps=c.eps,
                         name="in_proj_a")
    wb_cols = p["w_b"].shape[1]
    tn_b = 1024 if wb_cols % 1024 == 0 else 512
    proj_b = _rms_matmul(xf, 0, dm, p["norm_mix_w"], p["w_b"], BF16, tm=min(c.tm_in, seq),
                         tn=tn_b, eps=c.eps, name="in_proj_b", streams=2)
    col_kr = (c.q_lora + c.kv_lora) // LANES
    col_dt = col_kr + 1

    xbc = _conv_silu(proj_b, c.d_inner, p["conv_w"], p["conv_b"], bsz=bsz, seq=seq, cfg=c)
    y_f = _ssd_scan(xbc, proj_a, col_dt, p["dt_bias2"], p["a2"], reverse=False,
                    bsz=bsz, seq=seq, cfg=c)
    y_b = _ssd_scan(xbc, proj_a, col_dt, p["dt_bias2"], p["a2"], reverse=True,
                    bsz=bsz, seq=seq, cfg=c)

    cs, qtab = _rope_tabs(c, seq)
    assert c.q_lora == c.kv_lora
    qt = _rms_matmul_t(proj_a, 0, c.q_lora, p["q_norm_w"], p["wq2t"], BF16, tm=tm,
                       tn=p["wq2t"].shape[0], eps=c.eps, name="q_proj_t",
                       tab_t=qtab, seq=seq)
    kn = _rms_matmul(proj_a, 1, c.kv_lora, p["kv_norm_w"], p["wk"], BF16, tm=tm,
                     tn=p["wk"].shape[1], eps=c.eps, name="k_proj")
    vt = _rms_matmul_t(proj_a, 1, c.kv_lora, p["kv_norm_w"], p["wvt"], BF16, tm=tm,
                       tn=p["wvt"].shape[0], eps=c.eps, name="v_proj_t")
    kdup = _kdup(proj_a, col_kr, cs, seq=seq, tm=tm)
    attn = _flash(qt, kn, kdup, vt, bsz=bsz, seq=seq, cfg=c)
    ga_col0 = c.d_inner + conv_dim
    oa = _gated_matmul(attn, p["w_br_attn"], proj_b, ga_col0, tm=tm, tn=dm)

    merged = _ssm_out(y_f, y_b, xbc, proj_b, ga_col0 + dm, p["dskip_x"], p["ssm_norm_w"],
                      p["w_br_ssm"], oa, cfg=c)
    h, hn = _out_proj(xf, merged, p["w_out"], p["norm_ffn_w"], cfg=c)

    idx, gates, rank, cnt = _router(hn, p["w_router_t"], p["b_router"], cfg=c)
    dest, buf_tok, blk_e, blk_valid = _moe_plan(idx, rank, cnt[:, 0], n=n, cfg=c)
    xb = hn.at[buf_tok].get(mode="promise_in_bounds")
    yb = _experts(xb, blk_e, blk_valid, p["w_g"], p["w_u"], p["b_gu"], p["wd_a"], p["wd_b"],
                  p["b_down"], cfg=c)
    yk = yb.at[dest.reshape(-1)].get(mode="promise_in_bounds").reshape(c.top_k, n, dm)
    out = _combine(h, yk, gates.T, p["norm_final_w"], cfg=c)
    del hs
    return out.reshape(bsz, seq, dm)


def _forward(cfg, x_prompt, x_sample, *weights):
    depth = weights[0].shape[0]
    assert depth == 1
    names_per_layer = [w[0] for w in weights[:-1]]
    p = _prep(cfg, *names_per_layer, weights[-1])
    return _trunk(x_prompt, p, cfg), _trunk(x_sample, p, cfg)


def kernel(x_prompt, x_sample, norm_mix_w, w_in, q_norm_w, kv_norm_w, w_uq, w_ukv, conv_w, conv_b,
           dt_bias_f, dt_bias_b, a_log_f, a_log_b, d_skip, ssm_norm_w, w_br_attn, w_br_ssm,
           w_out, norm_ffn_w, w_router, b_router, w_gu, b_gu, w_down, b_down, norm_final_w):
    return _forward(Cfg(), x_prompt, x_sample, norm_mix_w, w_in, q_norm_w, kv_norm_w, w_uq, w_ukv,
                    conv_w, conv_b, dt_bias_f, dt_bias_b, a_log_f, a_log_b, d_skip, ssm_norm_w,
                    w_br_attn, w_br_ssm, w_out, norm_ffn_w, w_router, b_router, w_gu, b_gu,
                    w_down, b_down, norm_final_w)
```

```python
import functools
import math
from typing import NamedTuple

import jax
import jax.numpy as jnp
from jax import lax
from jax.experimental import pallas as pl
from jax.experimental.pallas import tpu as pltpu

F32 = jnp.float32
BF16 = jnp.bfloat16
LANES = 128
HALF = 64


class Cfg(NamedTuple):
    d_model: int = 2048
    n_heads: int = 16
    q_lora: int = 512
    kv_lora: int = 512
    qk_nope: int = 128
    qk_rope: int = 64
    v_head: int = 128
    rope_theta: float = 10000.0
    d_inner: int = 4096
    ssm_head_dim: int = 64
    n_groups: int = 8
    d_state: int = 128
    conv_k: int = 5
    chunk: int = 128
    n_experts: int = 32
    top_k: int = 4
    d_ff: int = 2048
    limit: float = 7.0
    alpha: float = 1.702
    eps: float = 1e-6
    tm: int = 512
    tm_in: int = 1024
    tq: int = 512
    tk: int = 512
    flash_unroll: int = 8
    flash_heads: int = 2
    t_conv: int = 512
    tc_conv: int = 2048
    tm_gate: int = 256
    tm_out: int = 256
    tm_route: int = 512
    tm_moe: int = 512
    tf_moe: int = 1024
    tm_comb: int = 256


def _cp(sem, vmem_mb=None):
    kw = dict(dimension_semantics=sem)
    if vmem_mb is not None:
        kw["vmem_limit_bytes"] = vmem_mb * 1024 * 1024
    return pltpu.CompilerParams(**kw)


def _rms_mm_kernel(x_ref, nw_ref, *rest, eps, streams):
    w_refs, (o_ref, xn_ref) = rest[:streams], rest[streams:]

    @pl.when(pl.program_id(1) == 0)
    def _():
        x = x_ref[...].astype(F32)
        ms = jnp.mean(x * x, axis=-1, keepdims=True)
        xn_ref[...] = (x * lax.rsqrt(ms + eps) * nw_ref[...]).astype(BF16)

    tw = o_ref.shape[1] // streams
    for s, w_ref in enumerate(w_refs):
        acc = jnp.dot(xn_ref[...], w_ref[...], preferred_element_type=F32)
        o_ref[:, s * tw:(s + 1) * tw] = acc.astype(o_ref.dtype)


def _rms_matmul(x, x_colblk, k, norm_w, w, out_dtype, *, tm, tn, eps, name, streams=1):
    n = x.shape[0]
    nout = w.shape[1]
    assert n % tm == 0 and nout % tn == 0 and w.shape[0] == k and tn % (streams * LANES) == 0
    tw = tn // streams
    in_specs = [
        pl.BlockSpec((tm, k), lambda i, j: (i, x_colblk)),
        pl.BlockSpec((1, k), lambda i, j: (0, 0)),
    ]
    for s in range(streams):
        in_specs.append(pl.BlockSpec((k, tw), lambda i, j, s=s: (0, j * streams + s)))
    return pl.pallas_call(
        functools.partial(_rms_mm_kernel, eps=eps, streams=streams),
        grid=(n // tm, nout // tn),
        in_specs=in_specs,
        out_specs=pl.BlockSpec((tm, tn), lambda i, j: (i, j)),
        out_shape=jax.ShapeDtypeStruct((n, nout), out_dtype),
        scratch_shapes=[pltpu.VMEM((tm, k), BF16)],
        compiler_params=_cp(("parallel", "arbitrary"), 48),
        name=name,
    )(x, norm_w.reshape(1, k).astype(F32), *([w] * streams))


def _rms_mm_t_kernel(x_ref, nw_ref, wt_ref, *rest, eps, tab_reps):
    if tab_reps:
        tab_ref, o_ref, xn_ref = rest
    else:
        o_ref, xn_ref = rest

    @pl.when(pl.program_id(1) == 0)
    def _():
        x = x_ref[...].astype(F32)
        ms = jnp.mean(x * x, axis=-1, keepdims=True)
        xn_ref[...] = (x * lax.rsqrt(ms + eps) * nw_ref[...]).astype(BF16)

    acc = lax.dot_general(wt_ref[...], xn_ref[...], (((1,), (1,)), ((), ())),
                          preferred_element_type=F32)
    if tab_reps:
        t = tab_ref[...]
        if tab_reps > 1:
            t = jnp.concatenate([t] * tab_reps, axis=0)
        acc = acc * t
    o_ref[...] = acc.astype(o_ref.dtype)


def _rms_matmul_t(x, x_colblk, k, norm_w, wt, out_dtype, *, tm, tn, eps, name, tab_t=None,
                  seq=None):
    n = x.shape[0]
    nout = wt.shape[0]
    assert n % tm == 0 and nout % tn == 0 and wt.shape[1] == k
    in_specs = [
        pl.BlockSpec((tm, k), lambda i, j: (i, x_colblk)),
        pl.BlockSpec((1, k), lambda i, j: (0, 0)),
        pl.BlockSpec((tn, k), lambda i, j: (j, 0)),
    ]
    args = [x, norm_w.reshape(1, k).astype(F32), wt]
    tab_reps = 0
    if tab_t is not None:
        tw = tab_t.shape[0]
        assert tn % tw == 0 and seq % tm == 0
        tab_reps = tn // tw
        nst = seq // tm
        in_specs.append(pl.BlockSpec((tw, tm), lambda i, j: (0, i % nst)))
        args.append(tab_t)
    return pl.pallas_call(
        functools.partial(_rms_mm_t_kernel, eps=eps, tab_reps=tab_reps),
        grid=(n // tm, nout // tn),
        in_specs=in_specs,
        out_specs=pl.BlockSpec((tn, tm), lambda i, j: (j, i)),
        out_shape=jax.ShapeDtypeStruct((nout, n), out_dtype),
        scratch_shapes=[pltpu.VMEM((tm, k), BF16)],
        compiler_params=_cp(("parallel", "arbitrary"), 48),
        name=name,
    )(*args)


HALO = 16


def _conv_kernel(prev_ref, x_ref, next_ref, w_ref, b_ref, o_ref, buf_ref, *, t, nt, kk):
    ti = pl.program_id(1)
    pv = prev_ref[...].astype(F32)[HALO - 8:, :]
    nx = next_ref[...].astype(F32)[:8, :]
    pv = jnp.where(ti == 0, 0.0, pv)
    nx = jnp.where(ti == nt - 1, 0.0, nx)
    buf_ref[0:8, :] = pv
    buf_ref[8:t + 8, :] = x_ref[...].astype(F32)
    buf_ref[t + 8:t + 16, :] = nx
    acc = b_ref[...]
    half = kk // 2
    for k in range(kk):
        acc = acc + w_ref[k:k + 1, :] * buf_ref[pl.ds(8 - half + k, t), :]
    o_ref[...] = (acc * jax.nn.sigmoid(acc)).astype(o_ref.dtype)


def _conv_silu(proj_b, col0, conv_w, conv_b, *, bsz, seq, cfg):
    n = proj_b.shape[0]
    cdim = conv_w.shape[1]
    t, tc = min(cfg.t_conv, seq), cfg.tc_conv
    assert seq % t == 0 and cdim % tc == 0 and col0 % tc == 0 and t % HALO == 0
    nt = seq // t
    cb0 = col0 // tc
    nrb = n // HALO
    rb = t // HALO

    def prev_map(b, ti, j):
        return (jnp.maximum((b * nt + ti) * rb - 1, 0), cb0 + j)

    def next_map(b, ti, j):
        return (jnp.minimum((b * nt + ti + 1) * rb, nrb - 1), cb0 + j)

    return pl.pallas_call(
        functools.partial(_conv_kernel, t=t, nt=nt, kk=cfg.conv_k),
        grid=(bsz, nt, cdim // tc),
        in_specs=[
            pl.BlockSpec((HALO, tc), prev_map),
            pl.BlockSpec((t, tc), lambda b, ti, j: (b * nt + ti, cb0 + j)),
            pl.BlockSpec((HALO, tc), next_map),
            pl.BlockSpec((cfg.conv_k, tc), lambda b, ti, j: (0, j)),
            pl.BlockSpec((1, tc), lambda b, ti, j: (0, j)),
        ],
        out_specs=pl.BlockSpec((t, tc), lambda b, ti, j: (b * nt + ti, j)),
        out_shape=jax.ShapeDtypeStruct((n, cdim), BF16),
        scratch_shapes=[pltpu.VMEM((t + 16, tc), F32)],
        compiler_params=_cp(("parallel", "parallel", "parallel")),
        name="conv_silu",
    )(proj_b, proj_b, proj_b, conv_w.astype(F32), conv_b.reshape(1, cdim).astype(F32))


def _ssd_kernel(xs_ref, b_ref, c_ref, dt_ref, bias_ref, a_ref, exp_ref, y_ref, st_ref, *,
                reverse, ll, ng, rr, pp, nn):
    assert pp == HALF and rr % 2 == 0 and 2 * ng * rr == LANES
    ci = pl.program_id(1)

    @pl.when(ci == 0)
    def _():
        st_ref[...] = jnp.zeros_like(st_ref)

    dcol = ng * rr if reverse else 0
    z = dt_ref[...] + bias_ref[...]
    dt_all = jnp.maximum(z, 0.0) + jnp.log(1.0 + jnp.exp(-jnp.abs(z)))
    dta = dt_all * a_ref[...]
    row = lax.broadcasted_iota(jnp.int32, (ll, ll), 0)
    col = lax.broadcasted_iota(jnp.int32, (ll, ll), 1)
    mask = (row <= col) if reverse else (row >= col)
    tri = mask.astype(F32)
    acum = jnp.dot(tri, dta, preferred_element_type=F32, precision=lax.Precision.HIGHEST)
    total = acum[0:1, :] if reverse else acum[ll - 1:ll, :]
    eac = jnp.exp(acum)
    wend = jnp.exp(total - acum) * dt_all
    etot = jnp.exp(total)
    acum_t = acum.T
    dt_t = dt_all.T
    lane = lax.broadcasted_iota(jnp.int32, (ll, LANES), 1)
    lo = lane < HALF
    v = jnp.concatenate([eac, wend, jnp.broadcast_to(etot, (16, LANES))], axis=0)
    v_hi = v.astype(BF16)
    v_lo = (v - v_hi.astype(F32)).astype(BF16)
    ex = jnp.dot(jnp.concatenate([v_hi, v_lo], axis=1), exp_ref[...],
                 preferred_element_type=F32)

    for g in range(ng):
        bg = b_ref[:, g * nn:(g + 1) * nn]
        cg = c_ref[:, g * nn:(g + 1) * nn]
        cb = lax.dot_general(cg, bg, (((1,), (1,)), ((), ())), preferred_element_type=F32)
        st = st_ref[g]
        gw = rr * pp
        yint = jnp.dot(cg, st.astype(BF16), preferred_element_type=F32)
        xw_parts = []
        for pr in range(rr // 2):
            c0 = dcol + g * rr + 2 * pr
            lanes0 = g * gw + pr * LANES
            xp = xs_ref[:, lanes0:lanes0 + LANES]
            ys = []
            for c in (c0, c0 + 1):
                seg = acum[:, c:c + 1] - acum_t[c:c + 1, :]
                dec = jnp.exp(jnp.where(mask, seg, -jnp.inf))
                m = (cb * dec * dt_t[c:c + 1, :]).astype(BF16)
                ys.append(jnp.dot(m, xp, preferred_element_type=F32))
            e_sel = ex[0:ll, lanes0:lanes0 + LANES]
            y = jnp.where(lo, ys[0], ys[1]) + yint[:, pr * LANES:(pr + 1) * LANES] * e_sel
            y_ref[:, lanes0:lanes0 + LANES] = y.astype(y_ref.dtype)
            w_sel = ex[ll:2 * ll, lanes0:lanes0 + LANES]
            xw_parts.append((xp.astype(F32) * w_sel).astype(BF16))
        xw = jnp.concatenate(xw_parts, axis=1)
        et = ex[2 * ll:2 * ll + 1, g * gw:(g + 1) * gw]
        upd = lax.dot_general(bg, xw, (((0,), (0,)), ((), ())), preferred_element_type=F32)
        st_ref[g] = st * et + upd


def _ssd_scan(xbc, dt_src, dt_colblk, bias2, a2, *, reverse, bsz, seq, cfg):
    n = xbc.shape[0]
    ll, ng, pp, nn = cfg.chunk, cfg.n_groups, cfg.ssm_head_dim, cfg.d_state
    di = cfg.d_inner
    rr = di // pp // ng
    nc = seq // ll
    gn = ng * nn
    assert di % gn == 0
    boff = di // gn

    def rblk(b, c):
        return b * nc + ((nc - 1 - c) if reverse else c)

    dcol = ng * rr if reverse else 0
    head_of_lane = jnp.arange(di, dtype=jnp.int32) // pp + dcol
    expand = (jnp.arange(LANES, dtype=jnp.int32)[:, None] == head_of_lane[None, :]).astype(BF16)
    expand2 = jnp.concatenate([expand, expand], axis=0)

    return pl.pallas_call(
        functools.partial(_ssd_kernel, reverse=reverse, ll=ll, ng=ng, rr=rr, pp=pp, nn=nn),
        grid=(bsz, nc),
        in_specs=[
            pl.BlockSpec((ll, di), lambda b, c: (rblk(b, c), 0)),
            pl.BlockSpec((ll, gn), lambda b, c: (rblk(b, c), boff)),
            pl.BlockSpec((ll, gn), lambda b, c: (rblk(b, c), boff + 1)),
            pl.BlockSpec((ll, LANES), lambda b, c: (rblk(b, c), dt_colblk)),
            pl.BlockSpec((1, LANES), lambda b, c: (0, 0)),
            pl.BlockSpec((1, LANES), lambda b, c: (0, 0)),
            pl.BlockSpec((2 * LANES, di), lambda b, c: (0, 0)),
        ],
        out_specs=pl.BlockSpec((ll, di), lambda b, c: (rblk(b, c), 0)),
        out_shape=jax.ShapeDtypeStruct((n, di), BF16),
        scratch_shapes=[pltpu.VMEM((ng, nn, rr * pp), F32)],
        compiler_params=_cp(("parallel", "arbitrary")),
        name="ssd_bwd" if reverse else "ssd_fwd",
    )(xbc, xbc, xbc, dt_src, bias2, a2, expand2)


def _ssm_out_kernel(yf_ref, yb_ref, xs_ref, z_ref, dsk_ref, nw_ref, w_ref, gs_ref, oa_ref,
                    o_ref, yn_ref, *, eps, ng):
    @pl.when(pl.program_id(1) == 0)
    def _():
        di = yn_ref.shape[1]
        gw = di // ng
        for g in range(ng):
            sl = slice(g * gw, (g + 1) * gw)
            y = (yf_ref[:, sl].astype(F32) + yb_ref[:, sl].astype(F32)
                 + dsk_ref[:, sl] * xs_ref[:, sl].astype(F32))
            zz = z_ref[:, sl].astype(F32)
            y = y * (zz * jax.nn.sigmoid(zz))
            ms = jnp.mean(y * y, axis=-1, keepdims=True)
            yn_ref[:, sl] = (y * lax.rsqrt(ms + eps) * nw_ref[:, sl]).astype(BF16)

    acc = jnp.dot(yn_ref[...], w_ref[...], preferred_element_type=F32)
    gate = jax.nn.sigmoid(gs_ref[...].astype(F32))
    o_ref[...] = (oa_ref[...].astype(F32) + gate * acc).astype(o_ref.dtype)


def _ssm_out(yf, yb, xbc, proj_b, gs_col0, dskip_x, norm_w, w_br, oa, *, cfg):
    n = yf.shape[0]
    di, dm = cfg.d_inner, cfg.d_model
    tm, tn = cfg.tm_gate, dm
    assert n % tm == 0 and dm % tn == 0 and gs_col0 % tn == 0
    gsb = gs_col0 // tn
    return pl.pallas_call(
        functools.partial(_ssm_out_kernel, eps=cfg.eps, ng=cfg.n_groups),
        grid=(n // tm, dm // tn),
        in_specs=[
            pl.BlockSpec((tm, di), lambda i, j: (i, 0)),
            pl.BlockSpec((tm, di), lambda i, j: (i, 0)),
            pl.BlockSpec((tm, di), lambda i, j: (i, 0)),
            pl.BlockSpec((tm, di), lambda i, j: (i, 0)),
            pl.BlockSpec((1, di), lambda i, j: (0, 0)),
            pl.BlockSpec((1, di), lambda i, j: (0, 0)),
            pl.BlockSpec((di, tn), lambda i, j: (0, j), pipeline_mode=pl.Buffered(1)),
            pl.BlockSpec((tm, tn), lambda i, j: (i, gsb + j)),
            pl.BlockSpec((tm, tn), lambda i, j: (i, j)),
        ],
        out_specs=pl.BlockSpec((tm, tn), lambda i, j: (i, j)),
        out_shape=jax.ShapeDtypeStruct((n, dm), BF16),
        scratch_shapes=[pltpu.VMEM((tm, di), BF16)],
        compiler_params=_cp(("parallel", "arbitrary"), 48),
        name="ssm_out",
    )(yf, yb, xbc, proj_b, dskip_x, norm_w, w_br, proj_b, oa)


def _gated_mm_kernel(x_ref, w_ref, g_ref, o_ref):
    acc = jnp.dot(x_ref[...], w_ref[...], preferred_element_type=F32)
    o_ref[...] = (jax.nn.sigmoid(g_ref[...].astype(F32)) * acc).astype(o_ref.dtype)


def _gated_matmul(x, w, gate_src, gate_col0, *, tm, tn):
    n, k = x.shape
    nout = w.shape[1]
    assert n % tm == 0 and nout % tn == 0 and gate_col0 % tn == 0
    gb = gate_col0 // tn
    return pl.pallas_call(
        _gated_mm_kernel,
        grid=(n // tm, nout // tn),
        in_specs=[
            pl.BlockSpec((tm, k), lambda i, j: (i, 0)),
            pl.BlockSpec((k, tn), lambda i, j: (0, j)),
            pl.BlockSpec((tm, tn), lambda i, j: (i, gb + j)),
        ],
        out_specs=pl.BlockSpec((tm, tn), lambda i, j: (i, j)),
        out_shape=jax.ShapeDtypeStruct((n, nout), BF16),
        compiler_params=_cp(("parallel", "arbitrary"), 48),
        name="attn_br",
    )(x, w, gate_src)


def _kdup_kernel(kr_ref, cs_ref, o_ref):
    k2 = kr_ref[...].astype(F32) * cs_ref[...]
    o_ref[...] = (k2 + pltpu.roll(k2, HALF, 1)).astype(o_ref.dtype)


def _kdup(proj_a, colblk, cs, *, seq, tm):
    n = proj_a.shape[0]
    tm = min(tm, seq)
    nst = seq // tm
    return pl.pallas_call(
        _kdup_kernel,
        grid=(n // tm,),
        in_specs=[
            pl.BlockSpec((tm, LANES), lambda i: (i, colblk)),
            pl.BlockSpec((tm, LANES), lambda i: (i % nst, 0)),
        ],
        out_specs=pl.BlockSpec((tm, LANES), lambda i: (i, 0)),
        out_shape=jax.ShapeDtypeStruct((n, LANES), BF16),
        compiler_params=_cp(("parallel",)),
        name="kdup",
    )(proj_a, cs)


ONES_ROWS = 16


def _flash_kernel(qt_ref, kn_ref, kd_ref, vt_ref, o_ref, s_ref, m_ref, acc_ref, *,
                  tk, nk, unroll, hps):
    dq = qt_ref.shape[0] // hps
    dv = vt_ref.shape[0] // hps
    dn = kn_ref.shape[1] // hps
    ones = jnp.ones((ONES_ROWS, tk), BF16)

    def scores(hd, c):
        off = pl.multiple_of(c * tk, tk)
        kcat = jnp.concatenate([kn_ref[pl.ds(off, tk), hd * dn:(hd + 1) * dn],
                                kd_ref[pl.ds(off, tk), :]], axis=1)
        return jnp.dot(kcat, qt_ref[hd * dq:(hd + 1) * dq, :],
                       preferred_element_type=F32)

    def update(hd, c, s):
        off = pl.multiple_of(c * tk, tk)
        m = m_ref[hd]
        m_new = jnp.maximum(m, jnp.max(s, axis=0, keepdims=True))
        alpha = jnp.exp2(m - m_new)
        p = jnp.exp2((s - m_new).astype(BF16))
        m_ref[hd] = m_new
        lhs = jnp.concatenate([vt_ref[hd * dv:(hd + 1) * dv, pl.ds(off, tk)], ones], axis=0)
        acc_ref[hd] = alpha * acc_ref[hd] + jnp.dot(lhs, p, preferred_element_type=F32)

    m_ref[...] = jnp.full(m_ref.shape, -jnp.inf, F32)
    acc_ref[...] = jnp.zeros(acc_ref.shape, F32)
    for hd in range(hps):
        s_ref[hd] = scores(hd, 0)

    def trip(c, last):
        s_cur = [s_ref[hd] for hd in range(hps)]
        for u in range(unroll):
            for hd in range(hps):
                s_next = None if (last and u == unroll - 1) else scores(hd, c + u + 1)
                update(hd, c + u, s_cur[hd])
                s_cur[hd] = s_next
        if not last:
            for hd in range(hps):
                s_ref[hd] = s_cur[hd]

    def body(ci, carry):
        trip(ci * unroll, False)
        return carry

    lax.fori_loop(0, nk // unroll - 1, body, 0)
    trip(nk - unroll, True)
    for hd in range(hps):
        o_ref[:, hd * dv:(hd + 1) * dv] = (
            acc_ref[hd, 0:dv, :] / acc_ref[hd, dv:dv + 1, :]).T.astype(o_ref.dtype)


def _flash(qt, kn, kdup, vt, *, bsz, seq, cfg):
    n = qt.shape[1]
    h = cfg.n_heads
    tq, tk = min(cfg.tq, seq), min(cfg.tk, seq)
    nq = seq // tq
    dq = qt.shape[0] // h
    dn, dv = cfg.qk_nope, cfg.v_head
    assert dn == LANES and dv == LANES and dq == 2 * LANES
    nk = seq // tk
    unroll = math.gcd(nk, cfg.flash_unroll)
    hps = math.gcd(h, cfg.flash_heads)
    return pl.pallas_call(
        functools.partial(_flash_kernel, tk=tk, nk=nk, unroll=unroll, hps=hps),
        grid=(bsz, h // hps, nq),
        in_specs=[
            pl.BlockSpec((hps * dq, tq), lambda b, hh, i: (hh, b * nq + i)),
            pl.BlockSpec((seq, hps * dn), lambda b, hh, i: (b, hh)),
            pl.BlockSpec((seq, LANES), lambda b, hh, i: (b, 0)),
            pl.BlockSpec((hps * dv, seq), lambda b, hh, i: (hh, b)),
        ],
        out_specs=pl.BlockSpec((tq, hps * dv), lambda b, hh, i: (b * nq + i, hh)),
        out_shape=jax.ShapeDtypeStruct((n, h * dv), BF16),
        scratch_shapes=[pltpu.VMEM((hps, tk, tq), F32), pltpu.VMEM((hps, 1, tq), F32),
                        pltpu.VMEM((hps, dv + ONES_ROWS, tq), F32)],
        compiler_params=_cp(("parallel", "parallel", "arbitrary"), 48),
        name="flash",
    )(qt, kn, kdup, vt)


def _out_kernel(x_ref, m_ref, w_ref, nw_ref, h_ref, hn_ref, *, eps):
    hh = x_ref[...] + jnp.dot(m_ref[...], w_ref[...], preferred_element_type=F32)
    h_ref[...] = hh
    ms = jnp.mean(hh * hh, axis=-1, keepdims=True)
    hn_ref[...] = (hh * lax.rsqrt(ms + eps) * nw_ref[...]).astype(hn_ref.dtype)


def _out_proj(x, merged, w_out, norm_w, *, cfg):
    n, dm = x.shape
    tm = cfg.tm_out
    return pl.pallas_call(
        functools.partial(_out_kernel, eps=cfg.eps),
        grid=(n // tm,),
        in_specs=[
            pl.BlockSpec((tm, dm), lambda i: (i, 0)),
            pl.BlockSpec((tm, dm), lambda i: (i, 0)),
            pl.BlockSpec((dm, dm), lambda i: (0, 0)),
            pl.BlockSpec((1, dm), lambda i: (0, 0)),
        ],
        out_specs=[pl.BlockSpec((tm, dm), lambda i: (i, 0)),
                   pl.BlockSpec((tm, dm), lambda i: (i, 0))],
        out_shape=[jax.ShapeDtypeStruct((n, dm), F32), jax.ShapeDtypeStruct((n, dm), BF16)],
        compiler_params=_cp(("parallel",), 48),
        name="out_proj",
    )(x, merged, w_out, norm_w)


def _router_kernel(hn_ref, wt_ref, b_ref, idx_ref, gate_ref, rank_ref, cnt_ref, base_ref, *,
                   ne, topk):
    @pl.when(pl.program_id(0) == 0)
    def _():
        base_ref[...] = jnp.zeros_like(base_ref)

    tm = hn_ref.shape[0]
    logits = lax.dot_general(wt_ref[...], hn_ref[...], (((1,), (1,)), ((), ())),
                             preferred_element_type=F32) + b_ref[...]
    eio = lax.broadcasted_iota(jnp.int32, (ne, tm), 0)
    work = logits
    vals, onehots = [], []
    for k in range(topk):
        mx = jnp.max(work, axis=0, keepdims=True)
        sel = jnp.min(jnp.where(work == mx, eio, ne), axis=0, keepdims=True)
        oh = eio == sel
        vals.append(mx)
        onehots.append(oh)
        idx_ref[k:k + 1, :] = sel
        work = jnp.where(oh, -jnp.inf, work)
    es = [jnp.exp(v - vals[0]) for v in vals]
    den = es[0]
    for e in es[1:]:
        den = den + e
    for k in range(topk):
        gate_ref[k:k + 1, :] = es[k] / den
    a = onehots[0].astype(F32)
    for oh in onehots[1:]:
        a = a + oh.astype(F32)
    r = lax.broadcasted_iota(jnp.int32, (tm, tm), 0)
    c = lax.broadcasted_iota(jnp.int32, (tm, tm), 1)
    su = (r < c).astype(BF16)
    cum = jnp.dot(a.astype(BF16), su, preferred_element_type=F32) + base_ref[:, 0:1]
    for k in range(topk):
        rk = jnp.sum(jnp.where(onehots[k], cum, 0.0), axis=0, keepdims=True)
        rank_ref[k:k + 1, :] = rk.astype(jnp.int32)
    base_ref[...] = base_ref[...] + jnp.sum(a, axis=1, keepdims=True)
    cnt_ref[...] = base_ref[...].astype(jnp.int32)


def _router(hn, w_router_t, b_router, *, cfg):
    n, dm = hn.shape
    ne, topk = cfg.n_experts, cfg.top_k
    tm = cfg.tm_route
    assert n % tm == 0
    return pl.pallas_call(
        functools.partial(_router_kernel, ne=ne, topk=topk),
        grid=(n // tm,),
        in_specs=[
            pl.BlockSpec((tm, dm), lambda i: (i, 0)),
            pl.BlockSpec((ne, dm), lambda i: (0, 0)),
            pl.BlockSpec((ne, 1), lambda i: (0, 0)),
        ],
        out_specs=[
            pl.BlockSpec((topk, tm), lambda i: (0, i)),
            pl.BlockSpec((topk, tm), lambda i: (0, i)),
            pl.BlockSpec((topk, tm), lambda i: (0, i)),
            pl.BlockSpec((ne, LANES), lambda i: (0, 0)),
        ],
        out_shape=[
            jax.ShapeDtypeStruct((topk, n), jnp.int32),
            jax.ShapeDtypeStruct((topk, n), F32),
            jax.ShapeDtypeStruct((topk, n), jnp.int32),
            jax.ShapeDtypeStruct((ne, LANES), jnp.int32),
        ],
        scratch_shapes=[pltpu.VMEM((ne, LANES), F32)],
        compiler_params=_cp(("arbitrary",)),
        name="router",
    )(hn, w_router_t, b_router)


def _expert_kernel(be_ref, nv_ref, x_ref, wg_ref, wu_ref, bg_ref, bu_ref, wda_ref, wdb_ref, bd_ref,
                   o_ref, acc_ref, *, limit, alpha, nj):
    i = pl.program_id(0)
    j = pl.program_id(1)
    nv = nv_ref[i]
    tm = x_ref.shape[0]
    half = tm // 2

    def mlp(rows):
        x = x_ref[0:rows, :]
        g = jnp.dot(x, wg_ref[...], preferred_element_type=F32) + bg_ref[...]
        u = jnp.dot(x, wu_ref[...], preferred_element_type=F32) + bu_ref[...]
        g = jnp.minimum(g, limit)
        u = jnp.clip(u, -limit, limit)
        hh = (g * jax.nn.sigmoid(alpha * g) * (u + 1.0)).astype(BF16)
        part = jnp.concatenate([jnp.dot(hh, wda_ref[...], preferred_element_type=F32),
                                jnp.dot(hh, wdb_ref[...], preferred_element_type=F32)], axis=1)

        @pl.when(j == 0)
        def _():
            acc_ref[0:rows, :] = part + bd_ref[...]

        @pl.when(j > 0)
        def _():
            acc_ref[0:rows, :] = acc_ref[0:rows, :] + part

    @pl.when(nv > half)
    def _():
        mlp(tm)

    @pl.when(jnp.logical_and(nv > 0, nv <= half))
    def _():
        mlp(half)

        @pl.when(j == 0)
        def _():
            acc_ref[half:tm, :] = jnp.zeros((tm - half, acc_ref.shape[1]), F32)

    @pl.when(j == nj - 1)
    def _():
        o_ref[...] = acc_ref[...].astype(o_ref.dtype)


def _experts(xb, blk_e, blk_valid, w_g, w_u, b_gu, wd_a, wd_b, b_down, *, cfg):
    rows, dm = xb.shape
    dff = cfg.d_ff
    tm, tf = cfg.tm_moe, min(cfg.tf_moe, dff)
    nj = dff // tf
    nblk = rows // tm
    dh = dm // 2
    grid_spec = pltpu.PrefetchScalarGridSpec(
        num_scalar_prefetch=2,
        grid=(nblk, nj),
        in_specs=[
            pl.BlockSpec((tm, dm), lambda i, j, be, nb: (i, 0)),
            pl.BlockSpec((None, dm, tf), lambda i, j, be, nb: (be[i], 0, j)),
            pl.BlockSpec((None, dm, tf), lambda i, j, be, nb: (be[i], 0, j)),
            pl.BlockSpec((None, 1, tf), lambda i, j, be, nb: (be[i], 0, j)),
            pl.BlockSpec((None, 1, tf), lambda i, j, be, nb: (be[i], 0, nj + j)),
            pl.BlockSpec((None, tf, dh), lambda i, j, be, nb: (be[i], j, 0)),
            pl.BlockSpec((None, tf, dh), lambda i, j, be, nb: (be[i], j, 0)),
            pl.BlockSpec((None, 1, dm), lambda i, j, be, nb: (be[i], 0, 0)),
        ],
        out_specs=pl.BlockSpec((tm, dm), lambda i, j, be, nb: (i, 0)),
        scratch_shapes=[pltpu.VMEM((tm, dm), F32)],
    )
    return pl.pallas_call(
        functools.partial(_expert_kernel, limit=cfg.limit, alpha=cfg.alpha, nj=nj),
        grid_spec=grid_spec,
        out_shape=jax.ShapeDtypeStruct((rows, dm), BF16),
        compiler_params=_cp(("arbitrary", "arbitrary"), 48),
        name="experts",
    )(blk_e, blk_valid, xb, w_g, w_u, b_gu, b_gu, wd_a, wd_b, b_down)


def _combine_kernel(h_ref, y_ref, g_ref, nw_ref, o_ref, *, eps, topk):
    hh = h_ref[...]
    g = g_ref[...]
    for k in range(topk):
        hh = hh + g[:, k:k + 1] * y_ref[k].astype(F32)
    ms = jnp.mean(hh * hh, axis=-1, keepdims=True)
    o_ref[...] = hh * lax.rsqrt(ms + eps) * nw_ref[...]


def _combine(h, yk, gates_nk, norm_w, *, cfg):
    n, dm = h.shape
    topk = cfg.top_k
    tm = cfg.tm_comb
    return pl.pallas_call(
        functools.partial(_combine_kernel, eps=cfg.eps, topk=topk),
        grid=(n // tm,),
        in_specs=[
            pl.BlockSpec((tm, dm), lambda i: (i, 0)),
            pl.BlockSpec((topk, tm, dm), lambda i: (0, i, 0)),
            pl.BlockSpec((tm, topk), lambda i: (i, 0)),
            pl.BlockSpec((1, dm), lambda i: (0, 0)),
        ],
        out_specs=pl.BlockSpec((tm, dm), lambda i: (i, 0)),
        out_shape=jax.ShapeDtypeStruct((n, dm), F32),
        compiler_params=_cp(("parallel",), 48),
        name="combine",
    )(h, yk, gates_nk, norm_w)


def _cast_kernel(xa_ref, xb_ref, oa_ref, ob_ref):
    oa_ref[...] = xa_ref[...].astype(oa_ref.dtype)
    ob_ref[...] = xb_ref[...].astype(ob_ref.dtype)


def _cast_split_bf16(w, *, block_bytes=4 * 1024 * 1024):
    e, r, c = w.shape
    ch = c // 2
    tr = max(8, min(r, block_bytes // (ch * 4)))
    assert r % tr == 0 and ch % LANES == 0
    half = jax.ShapeDtypeStruct((e, r, ch), BF16)
    return pl.pallas_call(
        _cast_kernel,
        grid=(e, r // tr),
        in_specs=[pl.BlockSpec((None, tr, ch), lambda i, j: (i, j, 0)),
                  pl.BlockSpec((None, tr, ch), lambda i, j: (i, j, 1))],
        out_specs=[pl.BlockSpec((None, tr, ch), lambda i, j: (i, j, 0)),
                   pl.BlockSpec((None, tr, ch), lambda i, j: (i, j, 0))],
        out_shape=[half, half],
        compiler_params=_cp(("parallel", "parallel"), 40),
        name="cast_bf16",
    )(w, w)


def _rot_cols(w):
    half = w.shape[-1] // 2
    return jnp.concatenate([-w[..., half:], w[..., :half]], axis=-1)


def _prep(cfg, norm_mix_w, w_in, q_norm_w, kv_norm_w, w_uq, w_ukv, conv_w, conv_b, dt_bias_f,
          dt_bias_b, a_log_f, a_log_b, d_skip, ssm_norm_w, w_br_attn, w_br_ssm, w_out,
          norm_ffn_w, w_router, b_router, w_gu, b_gu, w_down, b_down, norm_final_w):
    c = cfg
    hs = c.d_inner // c.ssm_head_dim
    conv_dim = c.d_inner + 2 * c.n_groups * c.d_state
    off_kv = c.q_lora
    off_kr = off_kv + c.kv_lora
    off_z = off_kr + c.qk_rope
    off_xbc = off_z + c.d_inner
    off_dtf = off_xbc + conv_dim
    off_dtb = off_dtf + hs
    off_ga = off_dtb + hs
    off_gs = off_ga + c.d_model
    w = w_in
    w_kr = w[:, off_kr:off_z]
    w_a = jnp.concatenate([w[:, :off_kr], w_kr, _rot_cols(w_kr), w[:, off_dtf:off_ga]], axis=1)
    w_b = jnp.concatenate([w[:, off_z:off_dtf], w[:, off_ga:]], axis=1)
    h = c.n_heads
    qk = c.qk_nope + c.qk_rope
    wq = w_uq.reshape(c.q_lora, h, qk)
    wq_r = wq[..., c.qk_nope:]
    wq2 = jnp.concatenate([wq[..., :c.qk_nope], wq_r, _rot_cols(wq_r)], axis=-1)
    wq2 = wq2.reshape(c.q_lora, h * (c.qk_nope + 2 * c.qk_rope))
    wkv = w_ukv.reshape(c.kv_lora, h, c.qk_nope + c.v_head)
    wk = wkv[..., :c.qk_nope].reshape(c.kv_lora, h * c.qk_nope)
    wvt = wkv[..., c.qk_nope:].reshape(c.kv_lora, h * c.v_head).T
    w_g, w_u = _cast_split_bf16(w_gu)
    wd_a, wd_b = _cast_split_bf16(w_down)
    return dict(
        w_g=w_g, w_u=w_u, wd_a=wd_a, wd_b=wd_b,
        norm_mix_w=norm_mix_w, w_a=w_a.astype(BF16), w_b=w_b.astype(BF16),
        q_norm_w=q_norm_w, kv_norm_w=kv_norm_w, wq2t=wq2.T.astype(BF16), wk=wk.astype(BF16),
        wvt=wvt.astype(BF16),
        conv_w=conv_w, conv_b=conv_b,
        dt_bias2=jnp.concatenate([dt_bias_f, dt_bias_b]).reshape(1, 2 * hs).astype(F32),
        a2=(-jnp.exp(jnp.concatenate([a_log_f, a_log_b]).astype(F32))).reshape(1, 2 * hs),
        dskip_x=jnp.repeat(d_skip.astype(F32), c.ssm_head_dim).reshape(1, c.d_inner),
        ssm_norm_w=ssm_norm_w.reshape(1, c.d_inner).astype(F32),
        w_br_attn=w_br_attn.astype(BF16), w_br_ssm=w_br_ssm.astype(BF16),
        w_out=w_out.astype(BF16), norm_ffn_w=norm_ffn_w.reshape(1, c.d_model).astype(F32),
        w_router_t=w_router.T.astype(BF16), b_router=b_router.reshape(c.n_experts, 1).astype(F32),
        b_gu=b_gu.reshape(c.n_experts, 1, 2 * c.d_ff).astype(F32),
        b_down=b_down.reshape(c.n_experts, 1, c.d_model).astype(F32),
        norm_final_w=norm_final_w.reshape(1, c.d_model).astype(F32),
    )


def _rope_tabs(cfg, seq):
    half = cfg.qk_rope // 2
    inv_freq = 1.0 / (cfg.rope_theta ** (jnp.arange(0, cfg.qk_rope, 2, dtype=F32) / cfg.qk_rope))
    ang = jnp.arange(seq, dtype=F32)[:, None] * inv_freq[None, :]
    ang = jnp.concatenate([ang, ang], axis=-1)
    cos, sin = jnp.cos(ang), jnp.sin(ang)
    del half
    cs = jnp.concatenate([cos, sin], axis=1)
    qscale = (cfg.qk_nope + cfg.qk_rope) ** -0.5 * math.log2(math.e)
    qtab = jnp.concatenate([jnp.ones((seq, cfg.qk_nope), F32), cs], axis=1) * qscale
    return cs, qtab.T


def _moe_plan(idx, rank, counts, *, n, cfg):
    ne, topk, tm = cfg.n_experts, cfg.top_k, cfg.tm_moe
    nk = n * topk
    nblk = -(-(nk + ne * (tm - 1)) // tm)
    rows = nblk * tm
    padded = (counts + tm - 1) // tm * tm
    pad_ends = jnp.cumsum(padded)
    pad_starts = pad_ends - padded
    dest = pad_starts[idx] + rank
    tok = jnp.broadcast_to(jnp.arange(n, dtype=jnp.int32)[None, :], (topk, n))
    buf_tok = jnp.zeros((rows,), jnp.int32).at[dest.reshape(-1)].set(
        tok.reshape(-1), mode="promise_in_bounds", unique_indices=True)
    blk_start = jnp.arange(nblk, dtype=jnp.int32) * tm
    blk_e = jnp.minimum(jnp.sum((pad_ends[None, :] <= blk_start[:, None]).astype(jnp.int32), axis=1),
                        ne - 1)
    blk_valid = jnp.clip(pad_starts[blk_e] + counts[blk_e] - blk_start, 0, tm).astype(jnp.int32)
    return dest, buf_tok, blk_e, blk_valid


def _trunk(x, p, cfg):
    c = cfg
    bsz, seq, dm = x.shape
    n = bsz * seq
    xf = x.reshape(n, dm)
    hs = c.d_inner // c.ssm_head_dim
    conv_dim = c.d_inner + 2 * c.n_groups * c.d_state
    tm = min(c.tm, seq)

    wa_cols = p["w_a"].shape[1]
    proj_a = _rms_matmul(xf, 0, dm, p["norm_mix_w"], p["w_a"], F32, tm=tm, tn=wa_cols, eps=c.eps,
                         name="in_proj_a")
    wb_cols = p["w_b"].shape[1]
    tn_b = 1024 if wb_cols % 1024 == 0 else 512
    proj_b = _rms_matmul(xf, 0, dm, p["norm_mix_w"], p["w_b"], BF16, tm=min(c.tm_in, seq),
                         tn=tn_b, eps=c.eps, name="in_proj_b", streams=2)
    col_kr = (c.q_lora + c.kv_lora) // LANES
    col_dt = col_kr + 1

    xbc = _conv_silu(proj_b, c.d_inner, p["conv_w"], p["conv_b"], bsz=bsz, seq=seq, cfg=c)
    y_f = _ssd_scan(xbc, proj_a, col_dt, p["dt_bias2"], p["a2"], reverse=False,
                    bsz=bsz, seq=seq, cfg=c)
    y_b = _ssd_scan(xbc, proj_a, col_dt, p["dt_bias2"], p["a2"], reverse=True,
                    bsz=bsz, seq=seq, cfg=c)

    cs, qtab = _rope_tabs(c, seq)
    assert c.q_lora == c.kv_lora
    qt = _rms_matmul_t(proj_a, 0, c.q_lora, p["q_norm_w"], p["wq2t"], BF16, tm=tm,
                       tn=p["wq2t"].shape[0], eps=c.eps, name="q_proj_t",
                       tab_t=qtab, seq=seq)
    kn = _rms_matmul(proj_a, 1, c.kv_lora, p["kv_norm_w"], p["wk"], BF16, tm=tm,
                     tn=p["wk"].shape[1], eps=c.eps, name="k_proj")
    vt = _rms_matmul_t(proj_a, 1, c.kv_lora, p["kv_norm_w"], p["wvt"], BF16, tm=tm,
                       tn=p["wvt"].shape[0], eps=c.eps, name="v_proj_t")
    kdup = _kdup(proj_a, col_kr, cs, seq=seq, tm=tm)
    attn = _flash(qt, kn, kdup, vt, bsz=bsz, seq=seq, cfg=c)
    ga_col0 = c.d_inner + conv_dim
    oa = _gated_matmul(attn, p["w_br_attn"], proj_b, ga_col0, tm=tm, tn=dm)

    merged = _ssm_out(y_f, y_b, xbc, proj_b, ga_col0 + dm, p["dskip_x"], p["ssm_norm_w"],
                      p["w_br_ssm"], oa, cfg=c)
    h, hn = _out_proj(xf, merged, p["w_out"], p["norm_ffn_w"], cfg=c)

    idx, gates, rank, cnt = _router(hn, p["w_router_t"], p["b_router"], cfg=c)
    dest, buf_tok, blk_e, blk_valid = _moe_plan(idx, rank, cnt[:, 0], n=n, cfg=c)
    xb = hn.at[buf_tok].get(mode="promise_in_bounds")
    yb = _experts(xb, blk_e, blk_valid, p["w_g"], p["w_u"], p["b_gu"], p["wd_a"], p["wd_b"],
                  p["b_down"], cfg=c)
    yk = yb.at[dest.reshape(-1)].get(mode="promise_in_bounds").reshape(c.top_k, n, dm)
    out = _combine(h, yk, gates.T, p["norm_final_w"], cfg=c)
    del hs
    return out.reshape(bsz, seq, dm)


def _forward(cfg, x_prompt, x_sample, *weights):
    depth = weights[0].shape[0]
    assert depth == 1
    names_per_layer = [w[0] for w in weights[:-1]]
    p = _prep(cfg, *names_per_layer, weights[-1])
    return _trunk(x_prompt, p, cfg), _trunk(x_sample, p, cfg)


def kernel(x_prompt, x_sample, norm_mix_w, w_in, q_norm_w, kv_norm_w, w_uq, w_ukv, conv_w, conv_b,
           dt_bias_f, dt_bias_b, a_log_f, a_log_b, d_skip, ssm_norm_w, w_br_attn, w_br_ssm,
           w_out, norm_ffn_w, w_router, b_router, w_gu, b_gu, w_down, b_down, norm_final_w):
    return _forward(Cfg(), x_prompt, x_sample, norm_mix_w, w_in, q_norm_w, kv_norm_w, w_uq, w_ukv,
                    conv_w, conv_b, dt_bias_f, dt_bias_b, a_log_f, a_log_b, d_skip, ssm_norm_w,
                    w_br_attn, w_br_ssm, w_out, norm_ffn_w, w_router, b_router, w_gu, b_gu,
                    w_down, b_down, norm_final_w)
```
